```python
import jax, jax.numpy as jnp
from jax import lax
import numpy as np

D_MODEL = 1024
BATCH = 8
SEQ = 4096
DEPTH = 4

N_MIXERS = 4
GROUP_WIDTH = D_MODEL // N_MIXERS
HEAD_DIM = 64
FOX_HEADS = GROUP_WIDTH // HEAD_DIM
RET_HEADS = GROUP_WIDTH // HEAD_DIM
POOL_WINDOWS = (2, 4, 8, 16)
POOL_GROUPS = len(POOL_WINDOWS)
POOL_GROUP_DIM = GROUP_WIDTH // POOL_GROUPS
CONV_WIDTH = 31
CONV_CH = GROUP_WIDTH
N_IN = 3 * GROUP_WIDTH + FOX_HEADS + GROUP_WIDTH + 4 * GROUP_WIDTH + 2 * CONV_CH
D_FF = ((8 * D_MODEL + 3 * 256 - 1) // (3 * 256)) * 256
Q_BLOCK = 128
RET_CHUNK = 128
ROPE_BASE = 10000.0
EPS = 1e-6

kernel_name = "hymba_style_fox_pool_retnet_conformer_trunk"


def rmsnorm(x, g):
    xf = x.astype(jnp.float32)
    y = xf * lax.rsqrt(jnp.mean(xf * xf, axis=-1, keepdims=True) + EPS)
    return (y * g.astype(jnp.float32)).astype(x.dtype)


def layernorm(x, g, b):
    xf = x.astype(jnp.float32)
    mu = jnp.mean(xf, axis=-1, keepdims=True)
    var = jnp.mean(jnp.square(xf - mu), axis=-1, keepdims=True)
    y = (xf - mu) * lax.rsqrt(var + EPS)
    return (y * g.astype(jnp.float32) + b.astype(jnp.float32)).astype(x.dtype)


def rope(x, positions):
    half = x.shape[-1] // 2
    inv = ROPE_BASE ** (-jnp.arange(half, dtype=jnp.float32) / half)
    ang = positions.astype(jnp.float32)[..., None] * inv
    cos = jnp.cos(ang)[:, :, None, :]
    sin = jnp.sin(ang)[:, :, None, :]
    x1, x2 = x[..., :half], x[..., half:]
    return jnp.concatenate([x1 * cos - x2 * sin, x1 * sin + x2 * cos], axis=-1).astype(x.dtype)


def fox_attention(q, k, v, f_logit, f_bias):
    B, S, H, Dh = q.shape
    log_f = jax.nn.log_sigmoid((f_logit + f_bias).astype(jnp.float32))
    cum = jnp.cumsum(log_f, axis=1).transpose(0, 2, 1)
    key_pos = jnp.arange(S)
    scale = Dh ** -0.5

    def block(i):
        start = i * Q_BLOCK
        qb = lax.dynamic_slice_in_dim(q, start, Q_BLOCK, axis=1)
        cq = lax.dynamic_slice_in_dim(cum, start, Q_BLOCK, axis=2)
        s = jnp.einsum('bqhd,bkhd->bhqk', qb, k).astype(jnp.float32) * scale
        s = s + cq[..., :, None] - cum[:, :, None, :]
        q_pos = start + jnp.arange(Q_BLOCK)
        causal = key_pos[None, :] <= q_pos[:, None]
        s = jnp.where(causal, s, -jnp.inf)
        p = jax.nn.softmax(s, axis=-1).astype(v.dtype)
        return jnp.einsum('bhqk,bkhd->bqhd', p, v)

    out = lax.map(block, jnp.arange(S // Q_BLOCK))
    return out.transpose(1, 0, 2, 3, 4).reshape(B, S, H * Dh)


def pool_mixer(u, pool_w, pool_scale):
    B, S, _ = u.shape
    ug = u.reshape(B, S, POOL_GROUPS, POOL_GROUP_DIM)
    cs = jnp.cumsum(ug.astype(jnp.float32), axis=1)
    pos = jnp.arange(S)
    means = []
    for gi, w in enumerate(POOL_WINDOWS):
        c_g = cs[:, :, gi]
        shifted = jnp.pad(c_g, ((0, 0), (w, 0), (0, 0)))[:, :S]
        count = jnp.minimum(pos + 1, w).astype(jnp.float32)[None, :, None]
        means.append((c_g - shifted) / count)
    pooled = jnp.stack(means, axis=2)
    delta = (pooled - ug.astype(jnp.float32)).astype(u.dtype)
    mixed = jnp.einsum('bsgc,gcd->bsgd', delta, pool_w).reshape(B, S, GROUP_WIDTH)
    return mixed * pool_scale


def retention(q, k, v, g, positions, gn_g):
    B, S, H, Dh = q.shape
    q = rope(q, positions)
    k = rope(k, positions) * (Dh ** -0.5)
    log_gamma = jnp.log(1.0 - 2.0 ** (-5.0 - jnp.arange(H, dtype=jnp.float32)))
    C = RET_CHUNK
    NC = S // C
    idx = jnp.arange(C, dtype=jnp.float32)
    diff = idx[:, None] - idx[None, :]
    dmask = jnp.where(diff >= 0, jnp.exp(log_gamma[:, None, None] * jnp.maximum(diff, 0.0)), 0.0)
    zeta = jnp.exp(log_gamma[:, None] * (C - 1 - idx))
    xi = jnp.exp(log_gamma[:, None] * (idx + 1))
    chunk_decay = jnp.exp(log_gamma * C)

    qc = q.reshape(B, NC, C, H, Dh)
    kc = k.reshape(B, NC, C, H, Dh)
    vc = v.reshape(B, NC, C, H, Dh)
    intra_s = jnp.einsum('bnqhd,bnkhd->bnhqk', qc, kc) * dmask
    intra = jnp.einsum('bnhqk,bnkhe->bnqhe', intra_s, vc)
    kv = jnp.einsum('bnkhd,hk,bnkhe->bnhde', kc, zeta, vc)

    def step(state, xs):
        q_n, kv_n = xs
        cross = jnp.einsum('bqhd,bhde->bqhe', q_n, state)
        new = (state * chunk_decay[None, :, None, None] + kv_n).astype(state.dtype)
        return new, cross

    state0 = jnp.zeros((B, H, Dh, Dh), dtype=kv.dtype)
    _, cross = lax.scan(step, state0, (qc.transpose(1, 0, 2, 3, 4), kv.transpose(1, 0, 2, 3, 4)))
    cross = cross.transpose(1, 0, 2, 3, 4) * xi.T[None, None, :, :, None]
    o = (intra + cross).reshape(B, S, H, Dh).astype(jnp.float32)
    mu = jnp.mean(o, axis=-1, keepdims=True)
    var = jnp.mean(jnp.square(o - mu), axis=-1, keepdims=True)
    o = (o - mu) * lax.rsqrt(var + EPS) * gn_g.reshape(H, Dh).astype(jnp.float32)
    o = o.reshape(B, S, H * Dh).astype(g.dtype)
    return jax.nn.silu(g) * o


def conformer_conv(u, conv_w, conv_b, ln_g, ln_b):
    a, gate = jnp.split(u, 2, axis=-1)
    h = a * jax.nn.sigmoid(gate)
    h = lax.conv_general_dilated(
        h, conv_w[:, None, :], window_strides=(1,), padding=[(CONV_WIDTH - 1, 0)],
        dimension_numbers=('NWC', 'WIO', 'NWC'), feature_group_count=CONV_CH) + conv_b
    return jax.nn.silu(layernorm(h, ln_g, ln_b))


def setup_inputs(seed: int = 0) -> dict:
    key = jax.random.key(seed)
    ks = jax.random.split(key, 24)
    f32 = jnp.float32
    D, G = D_MODEL, GROUP_WIDTH
    nrm = lambda k, shape, s: jax.random.normal(k, shape, f32) * s
    return {
        "x": jax.random.normal(ks[0], (BATCH, SEQ, D), f32),
        "c": jax.random.normal(ks[1], (BATCH, D), f32),
        "positions": jnp.broadcast_to(jnp.arange(SEQ, dtype=jnp.int32)[None, :], (BATCH, SEQ)),
        "ada_w": nrm(ks[2], (DEPTH, D, 6 * D), 0.5 * D ** -0.5),
        "ada_b": nrm(ks[3], (DEPTH, 6 * D), 0.01),
        "norm_mix_g": 1.0 + nrm(ks[4], (DEPTH, D), 0.1),
        "norm_ffn_g": 1.0 + nrm(ks[5], (DEPTH, D), 0.1),
        "w_in": nrm(ks[6], (DEPTH, D, N_IN), D ** -0.5),
        "fox_fb": jax.random.uniform(ks[7], (DEPTH, FOX_HEADS), f32, 2.0, 5.0),
        "pool_w": nrm(ks[8], (DEPTH, POOL_GROUPS, POOL_GROUP_DIM, POOL_GROUP_DIM), POOL_GROUP_DIM ** -0.5),
        "pool_scale": 1.0 + nrm(ks[9], (DEPTH, G), 0.1),
        "ret_gn_g": 1.0 + nrm(ks[10], (DEPTH, G), 0.1),
        "conv_w": nrm(ks[11], (DEPTH, CONV_WIDTH, CONV_CH), CONV_WIDTH ** -0.5),
        "conv_b": nrm(ks[12], (DEPTH, CONV_CH), 0.01),
        "conv_ln_g": 1.0 + nrm(ks[13], (DEPTH, CONV_CH), 0.1),
        "conv_ln_b": nrm(ks[14], (DEPTH, CONV_CH), 0.01),
        "w_out": nrm(ks[15], (DEPTH, D, D), D ** -0.5),
        "ffn_w1": nrm(ks[16], (DEPTH, D, D_FF), D ** -0.5),
        "ffn_w3": nrm(ks[17], (DEPTH, D, D_FF), D ** -0.5),
        "ffn_w2": nrm(ks[18], (DEPTH, D_FF, D), D_FF ** -0.5),
        "final_g": 1.0 + nrm(ks[19], (D,), 0.1),
    }


def reference(x, c, positions, ada_w, ada_b, norm_mix_g, norm_ffn_g, w_in, fox_fb,
              pool_w, pool_scale, ret_gn_g, conv_w, conv_b, conv_ln_g, conv_ln_b,
              w_out, ffn_w1, ffn_w3, ffn_w2, final_g):
    B, S, _ = x.shape
    G, H, Dh = GROUP_WIDTH, FOX_HEADS, HEAD_DIM
    bounds = [G, 2 * G, 3 * G, 3 * G + H,
              4 * G + H,
              5 * G + H, 6 * G + H, 7 * G + H, 8 * G + H]
    c_act = jax.nn.silu(c)
    for l in range(DEPTH):
        mod = (c_act @ ada_w[l] + ada_b[l])[:, None, :]
        sh1, sc1, g1, sh2, sc2, g2 = jnp.split(mod, 6, axis=-1)

        h = rmsnorm(x, norm_mix_g[l]) * (1.0 + sc1) + sh1
        u = h @ w_in[l]
        fq, fk, fv, ff, pu, rq, rk, rv, rg, cu = jnp.split(u, bounds, axis=-1)
        y_fox = fox_attention(fq.reshape(B, S, H, Dh), fk.reshape(B, S, H, Dh),
                              fv.reshape(B, S, H, Dh), ff, fox_fb[l])
        y_pool = pool_mixer(pu, pool_w[l], pool_scale[l])
        y_ret = retention(rq.reshape(B, S, RET_HEADS, Dh), rk.reshape(B, S, RET_HEADS, Dh),
                          rv.reshape(B, S, RET_HEADS, Dh), rg, positions, ret_gn_g[l])
        y_conv = conformer_conv(cu, conv_w[l], conv_b[l], conv_ln_g[l], conv_ln_b[l])
        mix = jnp.concatenate([y_fox, y_pool, y_ret, y_conv], axis=-1)
        x = x + g1 * (mix @ w_out[l])

        h = rmsnorm(x, norm_ffn_g[l]) * (1.0 + sc2) + sh2
        f = (jax.nn.silu(h @ ffn_w1[l]) * (h @ ffn_w3[l])) @ ffn_w2[l]
        x = x + g2 * f
    return rmsnorm(x, final_g)
```

```python
import functools
import math

import jax
import jax.numpy as jnp
from jax import lax
from jax.experimental import pallas as pl
from jax.experimental.pallas import tpu as pltpu

F32 = jnp.float32
BF16 = jnp.bfloat16

EPS = 1e-6
HEAD_DIM = 64
LANES = 128
GROUP = 256
POOL_WINDOWS = (2, 4, 8, 16)
POOL_HALO = 16
CONV_WIDTH = 31
CONV_HALO = 32
RET_CHUNK = 128
RET_HEADS = 4
ROPE_BASE = 10000.0
VMEM_LIMIT = 56 * 1024 * 1024

U32_WIDTH = 1920
COL_POOL, COL_RQ, COL_RK, COL_RV, COL_RG, COL_CA, COL_CG = 0, 1, 2, 3, 4, 5, 6
COL_FF_128 = 14


def _cparams(sem):
    return pltpu.CompilerParams(dimension_semantics=sem, vmem_limit_bytes=VMEM_LIMIT)


def _silu(x):
    return x * jax.nn.sigmoid(x)


def _rms_mod(x, g, scale, shift):
    ms = jnp.mean(x * x, axis=-1, keepdims=True)
    return (x * lax.rsqrt(ms + EPS) * g) * (1.0 + scale) + shift


def _dot(a, b):
    return jnp.dot(a, b, preferred_element_type=F32)


def _dot_nt(a, b):
    return lax.dot_general(a, b, (((1,), (1,)), ((), ())), preferred_element_type=F32)


def _dot_tn(a, b):
    return lax.dot_general(a, b, (((0,), (0,)), ((), ())), preferred_element_type=F32)


def _ada_kernel(c_ref, w_ref, b_ref, o_ref):
    ca = _silu(c_ref[...]).astype(BF16)
    o_ref[0] = _dot(ca, w_ref[0].astype(BF16)) + b_ref[0]


def _ada_mod(c, ada_w, ada_b):
    depth, d, n = ada_w.shape
    b = c.shape[0]
    tn = 1536
    return pl.pallas_call(
        _ada_kernel,
        out_shape=jax.ShapeDtypeStruct((depth, b, n), F32),
        grid=(depth, n // tn),
        in_specs=[pl.BlockSpec((b, d), lambda l, j: (0, 0)),
                  pl.BlockSpec((1, d, tn), lambda l, j: (l, 0, j)),
                  pl.BlockSpec((1, 1, tn), lambda l, j: (l, 0, j))],
        out_specs=pl.BlockSpec((1, b, tn), lambda l, j: (l, 0, j)),
        compiler_params=_cparams(("parallel", "parallel")),
        name="ada_mod",
    )(c, ada_w, ada_b.reshape(depth, 1, n))


def _rope_kernel(pos_ref, inv_ref, cos_ref, sin_ref):
    ang = pos_ref[0].astype(F32) * inv_ref[...]
    lane = lax.broadcasted_iota(jnp.int32, (1, LANES), 1)
    first_half = (lane % HEAD_DIM) < (HEAD_DIM // 2)
    cos_ref[0] = jnp.cos(ang)
    s = jnp.sin(ang)
    sin_ref[0] = jnp.where(first_half, -s, s)


def _rope_tables(positions):
    b, s = positions.shape
    half = HEAD_DIM // 2
    inv = ROPE_BASE ** (-jnp.arange(half, dtype=F32) / half)
    inv = jnp.tile(inv, LANES // half).reshape(1, LANES)
    ts = min(s, 512)
    return pl.pallas_call(
        _rope_kernel,
        out_shape=(jax.ShapeDtypeStruct((b, s, LANES), F32),) * 2,
        grid=(b, s // ts),
        in_specs=[pl.BlockSpec((1, ts, 1), lambda bi, i: (bi, i, 0)),
                  pl.BlockSpec((1, LANES), lambda bi, i: (0, 0))],
        out_specs=(pl.BlockSpec((1, ts, LANES), lambda bi, i: (bi, i, 0)),) * 2,
        compiler_params=_cparams(("parallel", "parallel")),
        name="rope_tables",
    )(positions.reshape(b, s, 1), inv)


def _inproj_kernel(x_ref, mod_ref, g_ref, w16_ref, w32_ref, u16_ref, u32_ref):
    h = _rms_mod(x_ref[0], g_ref[...], mod_ref[0, 1:2, :], mod_ref[0, 0:1, :]).astype(BF16)
    n16 = u16_ref.shape[-1]
    for c in range(0, n16, 256):
        u16_ref[0, :, c:c + 256] = _dot(h, w16_ref[:, c:c + 256]).astype(BF16)
    n32 = u32_ref.shape[-1]
    for c in range(0, n32, 384):
        u32_ref[0, :, c:c + 384] = _dot(h, w32_ref[:, c:c + 384])


def _inproj(x, mod_l, g, w16, w32, tm):
    b, s, d = x.shape
    n16, n32 = w16.shape[1], w32.shape[1]
    const = lambda bi, i: (0, 0)
    return pl.pallas_call(
        _inproj_kernel,
        out_shape=(jax.ShapeDtypeStruct((b, s, n16), BF16), jax.ShapeDtypeStruct((b, s, n32), F32)),
        grid=(b, s // tm),
        in_specs=[pl.BlockSpec((1, tm, d), lambda bi, i: (bi, i, 0)),
                  pl.BlockSpec((1, 6, d), lambda bi, i: (bi, 0, 0)),
                  pl.BlockSpec((1, d), const),
                  pl.BlockSpec((d, n16), const),
                  pl.BlockSpec((d, n32), const)],
        out_specs=(pl.BlockSpec((1, tm, n16), lambda bi, i: (bi, i, 0)),
                   pl.BlockSpec((1, tm, n32), lambda bi, i: (bi, i, 0))),
        compiler_params=_cparams(("parallel", "parallel")),
        name="in_proj",
    )(x, mod_l, g, w16, w32)


def _cum_kernel(ff_ref, fb_ref, o_ref, cum_ref, *, blk):
    s = ff_ref.shape[1]
    r = lax.broadcasted_iota(jnp.int32, (blk, blk), 0)
    c = lax.broadcasted_iota(jnp.int32, (blk, blk), 1)
    tri = (c <= r).astype(F32)

    def body(i, carry):
        r0 = pl.multiple_of(i * blk, blk)
        z = ff_ref[0, pl.ds(r0, blk), :] + fb_ref[...]
        lf = jnp.minimum(z, 0.0) - jnp.log1p(jnp.exp(-jnp.abs(z)))
        cum = jnp.dot(tri, lf, preferred_element_type=F32, precision=lax.Precision.HIGHEST) + carry
        cum_ref[pl.ds(r0, blk), :] = cum
        return cum[blk - 1:blk, :]

    lax.fori_loop(0, s // blk, body, jnp.zeros((1, LANES), F32))
    o_ref[0] = cum_ref[...].T[0:8, :]


def _forget_cumsum(u32, fb_row):
    b, s, _ = u32.shape
    blk = min(s, 256)
    return pl.pallas_call(
        functools.partial(_cum_kernel, blk=blk),
        out_shape=jax.ShapeDtypeStruct((b, 8, s), F32),
        grid=(b,),
        in_specs=[pl.BlockSpec((1, s, LANES), lambda bi: (bi, 0, COL_FF_128)),
                  pl.BlockSpec((1, LANES), lambda bi: (0, 0))],
        out_specs=pl.BlockSpec((1, 8, s), lambda bi: (bi, 0, 0)),
        scratch_shapes=[pltpu.VMEM((s, LANES), F32)],
        compiler_params=_cparams(("parallel",)),
        name="forget_cumsum",
    )(u32, fb_row)


def _fox_kernel(q_ref, k_ref, v_ref, ck_ref, o_ref, m_ref, l_ref, acc_ref, *, tq):
    hp = pl.program_id(1)
    i = pl.program_id(2)
    lane = lax.broadcasted_iota(jnp.int32, (1, LANES), 1)
    lo = lane < HEAD_DIM
    q = q_ref[0] * (HEAD_DIM ** -0.5)
    zero = jnp.zeros_like(q)
    q_heads = (jnp.where(lo, q, zero), jnp.where(lo, zero, q))
    m_ref[...] = jnp.full(m_ref.shape, -jnp.inf, F32)
    l_ref[...] = jnp.zeros(l_ref.shape, F32)
    acc_ref[...] = jnp.zeros(acc_ref.shape, F32)
    row = lax.broadcasted_iota(jnp.int32, (tq, tq), 0)
    col = lax.broadcasted_iota(jnp.int32, (tq, tq), 1)

    def step(j, diagonal):
        c0 = pl.multiple_of(j * tq, tq)
        k = k_ref[0, pl.ds(c0, tq), :]
        v = v_ref[0, pl.ds(c0, tq), :]
        alphas, pvs = [], []
        for hh in range(2):
            s = _dot_nt(q_heads[hh], k) - ck_ref[0, pl.ds(2 * hp + hh, 1), pl.ds(c0, tq)]
            if diagonal:
                s = jnp.where(col <= row, s, -jnp.inf)
            m_old = m_ref[hh]
            m_new = jnp.maximum(m_old, jnp.max(s, axis=-1, keepdims=True))
            alpha = jnp.exp(m_old - m_new)
            p = jnp.exp(s - m_new)
            l_ref[hh] = alpha * l_ref[hh] + jnp.sum(p, axis=-1, keepdims=True)
            m_ref[hh] = m_new
            alphas.append(alpha)
            pvs.append(_dot(p.astype(BF16), v))
        acc_ref[...] = (acc_ref[...] * jnp.where(lo, alphas[0], alphas[1])
                        + jnp.where(lo, pvs[0], pvs[1]))

    def off_diagonal(j, carry):
        step(j, False)
        return carry

    lax.fori_loop(0, i, off_diagonal, 0)
    step(i, True)
    o_ref[0] = (acc_ref[...] / jnp.where(lo, l_ref[0], l_ref[1])).astype(BF16)


def _fox_attention(u16, ck, tq):
    b, s, _ = u16.shape
    pairs = GROUP // LANES
    return pl.pallas_call(
        functools.partial(_fox_kernel, tq=tq),
        out_shape=jax.ShapeDtypeStruct((b, s, GROUP), BF16),
        grid=(b, pairs, s // tq),
        in_specs=[pl.BlockSpec((1, tq, LANES), lambda bi, hp, i: (bi, i, hp)),
                  pl.BlockSpec((1, s, LANES), lambda bi, hp, i: (bi, 0, pairs + hp)),
                  pl.BlockSpec((1, s, LANES), lambda bi, hp, i: (bi, 0, 2 * pairs + hp)),
                  pl.BlockSpec((1, 8, s), lambda bi, hp, i: (bi, 0, 0))],
        out_specs=pl.BlockSpec((1, tq, LANES), lambda bi, hp, i: (bi, i, hp)),
        scratch_shapes=[pltpu.VMEM((2, tq, 1), F32), pltpu.VMEM((2, tq, 1), F32),
                        pltpu.VMEM((tq, LANES), F32)],
        compiler_params=_cparams(("parallel", "parallel", "arbitrary")),
        name="fox_attention",
    )(u16, u16, u16, ck)


def _pool_kernel(u_ref, halo_ref, w_ref, scale_ref, o_ref, *, tm):
    i = pl.program_id(1)
    x = u_ref[0]
    halo = jnp.where(i > 0, halo_ref[0], 0.0)
    e = jnp.concatenate([halo, x], axis=0)
    sums = []
    acc = e
    for k in (1, 2, 4, 8):
        acc = acc + pltpu.roll(acc, k, 0)
        sums.append(acc[POOL_HALO:, :])
    lane = lax.broadcasted_iota(jnp.int32, (1, GROUP), 1)
    grp = lane // HEAD_DIM
    wsum = jnp.where(grp == 0, sums[0], jnp.where(grp == 1, sums[1], jnp.where(grp == 2, sums[2], sums[3])))
    width = jnp.where(grp == 0, POOL_WINDOWS[0],
                      jnp.where(grp == 1, POOL_WINDOWS[1],
                                jnp.where(grp == 2, POOL_WINDOWS[2], POOL_WINDOWS[3])))
    pos = i * tm + lax.broadcasted_iota(jnp.int32, (tm, 1), 0)
    count = jnp.minimum(pos + 1, width).astype(F32)
    delta = (wsum / count - x).astype(BF16)
    o_ref[0] = (_dot(delta, w_ref[...]) * scale_ref[...]).astype(BF16)


def _pool_mixer(u32, w_bd, scale, tm):
    b, s, _ = u32.shape
    hb = tm // POOL_HALO
    return pl.pallas_call(
        functools.partial(_pool_kernel, tm=tm),
        out_shape=jax.ShapeDtypeStruct((b, s, GROUP), BF16),
        grid=(b, s // tm),
        in_specs=[pl.BlockSpec((1, tm, GROUP), lambda bi, i: (bi, i, COL_POOL)),
                  pl.BlockSpec((1, POOL_HALO, GROUP), lambda bi, i: (bi, jnp.maximum(i * hb - 1, 0), COL_POOL)),
                  pl.BlockSpec((GROUP, GROUP), lambda bi, i: (0, 0)),
                  pl.BlockSpec((1, GROUP), lambda bi, i: (0, 0))],
        out_specs=pl.BlockSpec((1, tm, GROUP), lambda bi, i: (bi, i, 0)),
        compiler_params=_cparams(("parallel", "parallel")),
        name="pool_mixer",
    )(u32, u32, w_bd, scale)


def _conv_kernel(a_ref, g_ref, ha_ref, hg_ref, w_ref, b_ref, lng_ref, lnb_ref, o_ref, h_ref, *, tm, rows):
    i = pl.program_id(1)
    h_ref[0:CONV_HALO, :] = jnp.where(i > 0, ha_ref[0] * jax.nn.sigmoid(hg_ref[0]), 0.0)
    h_ref[CONV_HALO:, :] = a_ref[0] * jax.nn.sigmoid(g_ref[0])
    first = CONV_HALO - (CONV_WIDTH - 1)
    for r0 in range(0, tm, rows):
        acc = jnp.zeros((rows, GROUP), F32) + b_ref[...]
        for j in range(CONV_WIDTH):
            acc = acc + w_ref[j:j + 1, :] * h_ref[r0 + first + j:r0 + first + j + rows, :]
        mu = jnp.mean(acc, axis=-1, keepdims=True)
        d = acc - mu
        var = jnp.mean(d * d, axis=-1, keepdims=True)
        y = d * lax.rsqrt(var + EPS) * lng_ref[...] + lnb_ref[...]
        o_ref[0, r0:r0 + rows, :] = _silu(y).astype(BF16)


def _conv_mixer(u32, w, bias, ln_g, ln_b, tm):
    b, s, _ = u32.shape
    hb = tm // CONV_HALO
    rows = min(tm, 64)
    row = lambda bi, i: (0, 0)
    halo = lambda col: (lambda bi, i: (bi, jnp.maximum(i * hb - 1, 0), col))
    return pl.pallas_call(
        functools.partial(_conv_kernel, tm=tm, rows=rows),
        out_shape=jax.ShapeDtypeStruct((b, s, GROUP), BF16),
        grid=(b, s // tm),
        in_specs=[pl.BlockSpec((1, tm, GROUP), lambda bi, i: (bi, i, COL_CA)),
                  pl.BlockSpec((1, tm, GROUP), lambda bi, i: (bi, i, COL_CG)),
                  pl.BlockSpec((1, CONV_HALO, GROUP), halo(COL_CA)),
                  pl.BlockSpec((1, CONV_HALO, GROUP), halo(COL_CG)),
                  pl.BlockSpec((CONV_HALO, GROUP), row),
                  pl.BlockSpec((1, GROUP), row), pl.BlockSpec((1, GROUP), row), pl.BlockSpec((1, GROUP), row)],
        out_specs=pl.BlockSpec((1, tm, GROUP), lambda bi, i: (bi, i, 0)),
        scratch_shapes=[pltpu.VMEM((CONV_HALO + tm, GROUP), F32)],
        compiler_params=_cparams(("parallel", "parallel")),
        name="conv_mixer",
    )(u32, u32, u32, u32, w, bias, ln_g, ln_b)


def _ret_kernel(q_ref, k_ref, v_ref, g_ref, cos_ref, sin_ref, gn_ref, o_ref, state_ref, *, tr):
    C = RET_CHUNK

    @pl.when(pl.program_id(1) == 0)
    def _():
        state_ref[...] = jnp.zeros(state_ref.shape, F32)

    lane = lax.broadcasted_iota(jnp.int32, (1, LANES), 1)
    lo = lane < HEAD_DIM
    first_half = (lane % HEAD_DIM) < (HEAD_DIM // 2)
    ri = lax.broadcasted_iota(jnp.int32, (C, C), 0)
    ci = lax.broadcasted_iota(jnp.int32, (C, C), 1)
    diff = (ri - ci).astype(F32)
    tok = lax.broadcasted_iota(jnp.int32, (C, 1), 0).astype(F32)
    same_head = (ri < HEAD_DIM) == (ci < HEAD_DIM)
    seg_mean = jnp.where(same_head, 1.0 / HEAD_DIM, 0.0).astype(BF16)

    def seg_mean_dot(x):
        hi = x.astype(BF16)
        lo_part = (x - hi.astype(F32)).astype(BF16)
        return _dot(hi, seg_mean) + _dot(lo_part, seg_mean)

    cos = cos_ref[0]
    sin = sin_ref[0]

    def rope(x):
        swapped = jnp.where(first_half, pltpu.roll(x, LANES - HEAD_DIM // 2, 1), pltpu.roll(x, HEAD_DIM // 2, 1))
        return x * cos + swapped * sin

    for pr in range(RET_HEADS // 2):
        lg = [math.log(1.0 - 2.0 ** (-5.0 - (2 * pr + hh))) for hh in range(2)]
        dmask = [jnp.where(diff >= 0, jnp.exp(lg[hh] * jnp.maximum(diff, 0.0)), 0.0) for hh in range(2)]
        zeta = jnp.where(lo, jnp.exp(lg[0] * (C - 1 - tok)), jnp.exp(lg[1] * (C - 1 - tok)))
        xi = jnp.where(lo, jnp.exp(lg[0] * (tok + 1)), jnp.exp(lg[1] * (tok + 1)))
        row_is_lo = lax.broadcasted_iota(jnp.int32, (LANES, 1), 0) < HEAD_DIM
        decay = jnp.where(row_is_lo, math.exp(lg[0] * C), math.exp(lg[1] * C))
        cols = slice(pr * LANES, (pr + 1) * LANES)
        qr = rope(q_ref[0, :, cols]).astype(BF16)
        kf = rope(k_ref[0, :, cols]) * (HEAD_DIM ** -0.5)
        state = state_ref[pr]
        for n in range(tr // C):
            rows = slice(n * C, (n + 1) * C)
            qc = qr[rows]
            kc = kf[rows]
            kb = kc.astype(BF16)
            vb = v_ref[0, rows, cols].astype(BF16)
            zero = jnp.zeros_like(qc)
            intra = []
            for hh in range(2):
                qh = jnp.where(lo, qc, zero) if hh == 0 else jnp.where(lo, zero, qc)
                sc = (_dot_nt(qh, kb) * dmask[hh]).astype(BF16)
                intra.append(_dot(sc, vb))
            cross = _dot(qc, state.astype(BF16)) * xi
            o = jnp.where(lo, intra[0], intra[1]) + cross
            kv = _dot_tn((kc * zeta).astype(BF16), vb)
            state = state * decay + jnp.where(same_head, kv, 0.0)
            mu = seg_mean_dot(o)
            d = o - mu
            var = seg_mean_dot(d * d)
            y = d * lax.rsqrt(var + EPS) * gn_ref[:, cols]
            o_ref[0, rows, cols] = (_silu(g_ref[0, rows, cols]) * y).astype(BF16)
        state_ref[pr] = state


def _retention(u32, cos_t, sin_t, gn_g, tr):
    b, s, _ = u32.shape
    blk = lambda col: pl.BlockSpec((1, tr, GROUP), lambda bi, i: (bi, i, col))
    tab = pl.BlockSpec((1, tr, LANES), lambda bi, i: (bi, i, 0))
    return pl.pallas_call(
        functools.partial(_ret_kernel, tr=tr),
        out_shape=jax.ShapeDtypeStruct((b, s, GROUP), BF16),
        grid=(b, s // tr),
        in_specs=[blk(COL_RQ), blk(COL_RK), blk(COL_RV), blk(COL_RG), tab, tab,
                  pl.BlockSpec((1, GROUP), lambda bi, i: (0, 0))],
        out_specs=pl.BlockSpec((1, tr, GROUP), lambda bi, i: (bi, i, 0)),
        scratch_shapes=[pltpu.VMEM((RET_HEADS // 2, LANES, LANES), F32)],
        compiler_params=_cparams(("parallel", "arbitrary")),
        name="retention",
    )(u32, u32, u32, u32, cos_t, sin_t, gn_g)


def _dense_kernel(x_ref, yf_ref, yp_ref, yr_ref, yc_ref, mod_ref, g_ref, fg_ref,
                  wo_ref, w1_ref, w3_ref, w2_ref, o_ref, *, ff_chunk, final):
    mix = _dot(yf_ref[0], wo_ref[0:GROUP, :])
    for n, y_ref in enumerate((yp_ref, yr_ref, yc_ref), start=1):
        mix = mix + _dot(y_ref[0], wo_ref[n * GROUP:(n + 1) * GROUP, :])
    x = x_ref[0] + mod_ref[0, 2:3, :] * mix
    h = _rms_mod(x, g_ref[...], mod_ref[0, 4:5, :], mod_ref[0, 3:4, :]).astype(BF16)
    d_ff = w1_ref.shape[1]
    f = None
    for c in range(0, d_ff, ff_chunk):
        a = _dot(h, w1_ref[:, c:c + ff_chunk])
        gated = (_silu(a) * _dot(h, w3_ref[:, c:c + ff_chunk])).astype(BF16)
        part = _dot(gated, w2_ref[c:c + ff_chunk, :])
        f = part if f is None else f + part
    x = x + mod_ref[0, 5:6, :] * f
    if final:
        ms = jnp.mean(x * x, axis=-1, keepdims=True)
        x = x * lax.rsqrt(ms + EPS) * fg_ref[...]
    o_ref[0] = x


def _dense(x, ys, mod_l, g, final_g, wo, w1, w3, w2, tm, final):
    b, s, d = x.shape
    d_ff = w1.shape[1]
    ff_chunk = 256
    assert d_ff % ff_chunk == 0
    const = lambda bi, i: (0, 0)
    resident = lambda shape: pl.BlockSpec(shape, const, pipeline_mode=pl.Buffered(1))
    tile = lambda w: pl.BlockSpec((1, tm, w), lambda bi, i: (bi, i, 0))
    return pl.pallas_call(
        functools.partial(_dense_kernel, ff_chunk=ff_chunk, final=final),
        out_shape=jax.ShapeDtypeStruct((b, s, d), F32),
        grid=(b, s // tm),
        in_specs=[tile(d), tile(GROUP), tile(GROUP), tile(GROUP), tile(GROUP),
                  pl.BlockSpec((1, 6, d), lambda bi, i: (bi, 0, 0)),
                  pl.BlockSpec((1, d), const), pl.BlockSpec((1, d), const),
                  resident((d, d)), resident((d, d_ff)), resident((d, d_ff)), resident((d_ff, d))],
        out_specs=tile(d),
        compiler_params=_cparams(("parallel", "parallel")),
        name="out_proj_swiglu",
    )(x, *ys, mod_l, g, final_g, wo, w1, w3, w2)


def kernel(x, c, positions, ada_w, ada_b, norm_mix_g, norm_ffn_g, w_in, fox_fb, pool_w, pool_scale,
           ret_gn_g, conv_w, conv_b, conv_ln_g, conv_ln_b, w_out, ffn_w1, ffn_w3, ffn_w2, final_g):
    b, s, d = x.shape
    depth = ada_w.shape[0]
    tm = min(s, 512)
    n_fox = 3 * GROUP
    n_ff = fox_fb.shape[1]

    mod = _ada_mod(c, ada_w, ada_b).reshape(depth, b, 6, d)
    cos_t, sin_t = _rope_tables(positions)

    for l in range(depth):
        w16 = w_in[l][:, :n_fox].astype(BF16)
        w32 = jnp.concatenate(
            [w_in[l][:, n_fox + n_ff:], w_in[l][:, n_fox:n_fox + n_ff],
             jnp.zeros((d, LANES - n_ff), F32)], axis=1).astype(BF16)
        assert w32.shape[1] == U32_WIDTH
        u16, u32 = _inproj(x, mod[l], norm_mix_g[l].reshape(1, d), w16, w32, tm)

        fb_row = jnp.pad(fox_fb[l], (0, LANES - n_ff)).reshape(1, LANES)
        ck = _forget_cumsum(u32, fb_row)
        y_fox = _fox_attention(u16, ck, tm)

        pool_bd = jax.scipy.linalg.block_diag(*[pool_w[l][gi] for gi in range(len(POOL_WINDOWS))]).astype(BF16)
        y_pool = _pool_mixer(u32, pool_bd, pool_scale[l].reshape(1, GROUP), tm)

        y_ret = _retention(u32, cos_t, sin_t, ret_gn_g[l].reshape(1, GROUP), tm)

        conv_w_pad = jnp.pad(conv_w[l], ((0, CONV_HALO - CONV_WIDTH), (0, 0)))
        y_conv = _conv_mixer(u32, conv_w_pad, conv_b[l].reshape(1, GROUP), conv_ln_g[l].reshape(1, GROUP),
                             conv_ln_b[l].reshape(1, GROUP), tm)

        x = _dense(x, (y_fox, y_pool, y_ret, y_conv), mod[l], norm_ffn_g[l].reshape(1, d),
                   final_g.reshape(1, d), w_out[l].astype(BF16), ffn_w1[l].astype(BF16),
                   ffn_w3[l].astype(BF16), ffn_w2[l].astype(BF16), tm, final=(l == depth - 1))
    return x
```

```python
import functools
import math

import jax
import jax.numpy as jnp
from jax import lax
from jax.experimental import pallas as pl
from jax.experimental.pallas import tpu as pltpu

F32 = jnp.float32
BF16 = jnp.bfloat16

EPS = 1e-6
HEAD_DIM = 64
LANES = 128
GROUP = 256
POOL_WINDOWS = (2, 4, 8, 16)
POOL_HALO = 16
CONV_WIDTH = 31
CONV_HALO = 32
RET_CHUNK = 128
RET_HEADS = 4
ROPE_BASE = 10000.0
VMEM_LIMIT = 56 * 1024 * 1024

U32_WIDTH = 1920
COL_POOL, COL_RQ, COL_RK, COL_RV, COL_RG, COL_CA, COL_CG = 0, 1, 2, 3, 4, 5, 6
COL_FF_128 = 14


def _cparams(sem):
    return pltpu.CompilerParams(dimension_semantics=sem, vmem_limit_bytes=VMEM_LIMIT)


def _silu(x):
    return x * jax.nn.sigmoid(x)


def _rms_mod(x, g, scale, shift):
    ms = jnp.mean(x * x, axis=-1, keepdims=True)
    return (x * lax.rsqrt(ms + EPS) * g) * (1.0 + scale) + shift


def _dot(a, b):
    return jnp.dot(a, b, preferred_element_type=F32)


def _dot_nt(a, b):
    return lax.dot_general(a, b, (((1,), (1,)), ((), ())), preferred_element_type=F32)


def _dot_tn(a, b):
    return lax.dot_general(a, b, (((0,), (0,)), ((), ())), preferred_element_type=F32)


def _ada_kernel(c_ref, w_ref, b_ref, o_ref):
    ca = _silu(c_ref[...]).astype(BF16)
    o_ref[0] = _dot(ca, w_ref[0].astype(BF16)) + b_ref[0]


def _ada_mod(c, ada_w, ada_b):
    depth, d, n = ada_w.shape
    b = c.shape[0]
    tn = 1536
    return pl.pallas_call(
        _ada_kernel,
        out_shape=jax.ShapeDtypeStruct((depth, b, n), F32),
        grid=(depth, n // tn),
        in_specs=[pl.BlockSpec((b, d), lambda l, j: (0, 0)),
                  pl.BlockSpec((1, d, tn), lambda l, j: (l, 0, j)),
                  pl.BlockSpec((1, 1, tn), lambda l, j: (l, 0, j))],
        out_specs=pl.BlockSpec((1, b, tn), lambda l, j: (l, 0, j)),
        compiler_params=_cparams(("parallel", "parallel")),
        name="ada_mod",
    )(c, ada_w, ada_b.reshape(depth, 1, n))


def _rope_kernel(pos_ref, inv_ref, cos_ref, sin_ref):
    ang = pos_ref[0].astype(F32) * inv_ref[...]
    lane = lax.broadcasted_iota(jnp.int32, (1, LANES), 1)
    first_half = (lane % HEAD_DIM) < (HEAD_DIM // 2)
    cos_ref[0] = jnp.cos(ang)
    s = jnp.sin(ang)
    sin_ref[0] = jnp.where(first_half, -s, s)


def _rope_tables(positions):
    b, s = positions.shape
    half = HEAD_DIM // 2
    inv = ROPE_BASE ** (-jnp.arange(half, dtype=F32) / half)
    inv = jnp.tile(inv, LANES // half).reshape(1, LANES)
    ts = min(s, 512)
    return pl.pallas_call(
        _rope_kernel,
        out_shape=(jax.ShapeDtypeStruct((b, s, LANES), F32),) * 2,
        grid=(b, s // ts),
        in_specs=[pl.BlockSpec((1, ts, 1), lambda bi, i: (bi, i, 0)),
                  pl.BlockSpec((1, LANES), lambda bi, i: (0, 0))],
        out_specs=(pl.BlockSpec((1, ts, LANES), lambda bi, i: (bi, i, 0)),) * 2,
        compiler_params=_cparams(("parallel", "parallel")),
        name="rope_tables",
    )(positions.reshape(b, s, 1), inv)


def _inproj_kernel(x_ref, mod_ref, g_ref, w16_ref, wvt_ref, w32_ref, u16_ref, vt_ref, u32_ref):
    h = _rms_mod(x_ref[0], g_ref[...], mod_ref[0, 1:2, :], mod_ref[0, 0:1, :]).astype(BF16)
    n16 = u16_ref.shape[-1]
    for c in range(0, n16, 256):
        u16_ref[0, :, c:c + 256] = _dot(h, w16_ref[:, c:c + 256]).astype(BF16)
    vt_ref[0] = _dot_nt(wvt_ref[...], h).astype(BF16)
    n32 = u32_ref.shape[-1]
    for c in range(0, n32, 384):
        u32_ref[0, :, c:c + 384] = _dot(h, w32_ref[:, c:c + 384])


def _inproj(x, mod_l, g, w16, wvt, w32, tm):
    b, s, d = x.shape
    n16, n32 = w16.shape[1], w32.shape[1]
    const = lambda bi, i: (0, 0)
    return pl.pallas_call(
        _inproj_kernel,
        out_shape=(jax.ShapeDtypeStruct((b, s, n16), BF16), jax.ShapeDtypeStruct((b, GROUP, s), BF16),
                   jax.ShapeDtypeStruct((b, s, n32), F32)),
        grid=(b, s // tm),
        in_specs=[pl.BlockSpec((1, tm, d), lambda bi, i: (bi, i, 0)),
                  pl.BlockSpec((1, 6, d), lambda bi, i: (bi, 0, 0)),
                  pl.BlockSpec((1, d), const),
                  pl.BlockSpec((d, n16), const),
                  pl.BlockSpec((GROUP, d), const),
                  pl.BlockSpec((d, n32), const)],
        out_specs=(pl.BlockSpec((1, tm, n16), lambda bi, i: (bi, i, 0)),
                   pl.BlockSpec((1, GROUP, tm), lambda bi, i: (bi, 0, i)),
                   pl.BlockSpec((1, tm, n32), lambda bi, i: (bi, i, 0))),
        compiler_params=_cparams(("parallel", "parallel")),
        name="in_proj",
    )(x, mod_l, g, w16, wvt, w32)


BIAS_TERMS = 3


def _cum_kernel(ff_ref, fb_ref, k_ref, kx_ref, *, blk):
    s = ff_ref.shape[1]
    r = lax.broadcasted_iota(jnp.int32, (blk, blk), 0)
    c = lax.broadcasted_iota(jnp.int32, (blk, blk), 1)
    tri = (c <= r).astype(F32)
    pr = lax.broadcasted_iota(jnp.int32, (LANES, LANES), 0)
    pc = lax.broadcasted_iota(jnp.int32, (LANES, LANES), 1)

    def place(pair, term):
        hit = ((pr == 2 * pair) & (pc == term)) | ((pr == 2 * pair + 1) & (pc == BIAS_TERMS + term))
        return jnp.where(hit, 1.0, 0.0).astype(BF16)

    def body(i, carry):
        r0 = pl.multiple_of(i * blk, blk)
        z = ff_ref[0, pl.ds(r0, blk), :] + fb_ref[...]
        lf = jnp.minimum(z, 0.0) - jnp.log1p(jnp.exp(-jnp.abs(z)))
        cum = jnp.dot(tri, lf, preferred_element_type=F32, precision=lax.Precision.HIGHEST) + carry
        terms = []
        rest = -cum
        for _ in range(BIAS_TERMS):
            t = rest.astype(BF16)
            terms.append(t)
            rest = rest - t.astype(F32)
        for pair in range(GROUP // LANES):
            bias = _dot(terms[0], place(pair, 0))
            for t in range(1, BIAS_TERMS):
                bias = bias + _dot(terms[t], place(pair, t))
            base = 2 * pair * LANES
            kx_ref[0, pl.ds(r0, blk), base:base + LANES] = k_ref[0, pl.ds(r0, blk), pair * LANES:(pair + 1) * LANES]
            kx_ref[0, pl.ds(r0, blk), base + LANES:base + 2 * LANES] = bias.astype(BF16)
        return cum[blk - 1:blk, :]

    lax.fori_loop(0, s // blk, body, jnp.zeros((1, LANES), F32))


def _forget_keys(u32, u16, fb_row):
    b, s, _ = u32.shape
    blk = min(s, 256)
    return pl.pallas_call(
        functools.partial(_cum_kernel, blk=blk),
        out_shape=jax.ShapeDtypeStruct((b, s, 2 * GROUP), BF16),
        grid=(b,),
        in_specs=[pl.BlockSpec((1, s, LANES), lambda bi: (bi, 0, COL_FF_128)),
                  pl.BlockSpec((1, LANES), lambda bi: (0, 0)),
                  pl.BlockSpec((1, s, GROUP), lambda bi: (bi, 0, 1))],
        out_specs=pl.BlockSpec((1, s, 2 * GROUP), lambda bi: (bi, 0, 0)),
        compiler_params=_cparams(("parallel",)),
        name="forget_keys",
    )(u32, fb_row, u16)


def _fox_kernel(q_ref, kx_ref, vt_ref, o_ref, m_ref, l_ref, acc_ref, *, tq):
    i = pl.program_id(2)
    lane = lax.broadcasted_iota(jnp.int32, (1, LANES), 1)
    lo = lane < HEAD_DIM
    q = q_ref[0] * (HEAD_DIM ** -0.5)
    qx = []
    for hh in range(2):
        own = jnp.where(lo == (hh == 0), 1.0, 0.0).astype(BF16)
        picks = jnp.where(lane // BIAS_TERMS == hh, 1.0, 0.0).astype(BF16)
        qx.append(jnp.concatenate([q * own, jnp.broadcast_to(picks, q.shape)], axis=1))
    m_ref[...] = jnp.full(m_ref.shape, -jnp.inf, F32)
    l_ref[...] = jnp.zeros(l_ref.shape, F32)
    acc_ref[...] = jnp.zeros(acc_ref.shape, F32)
    key = lax.broadcasted_iota(jnp.int32, (tq, tq), 0)
    qry = lax.broadcasted_iota(jnp.int32, (tq, tq), 1)

    def step(j, diagonal):
        c0 = pl.multiple_of(j * tq, tq)
        kx = kx_ref[0, pl.ds(c0, tq), :]
        for hh in range(2):
            dims = slice(hh * HEAD_DIM, (hh + 1) * HEAD_DIM)
            s = _dot_nt(kx, qx[hh])
            if diagonal:
                s = jnp.where(key <= qry, s, -jnp.inf)
            m_old = m_ref[hh]
            m_new = jnp.maximum(m_old, jnp.max(s, axis=0, keepdims=True))
            alpha = jnp.exp(m_old - m_new)
            p = jnp.exp(s - m_new)
            l_ref[hh] = alpha * l_ref[hh] + jnp.sum(p, axis=0, keepdims=True)
            m_ref[hh] = m_new
            pv = _dot(vt_ref[0, dims, pl.ds(c0, tq)], p.astype(BF16))
            acc_ref[dims, :] = acc_ref[dims, :] * alpha + pv

    def off_diagonal(j, carry):
        step(j, False)
        return carry

    lax.fori_loop(0, i, off_diagonal, 0)
    step(i, True)
    out_t = jnp.concatenate([acc_ref[0:HEAD_DIM, :] / l_ref[0], acc_ref[HEAD_DIM:, :] / l_ref[1]], axis=0)
    o_ref[0] = out_t.T.astype(BF16)


def _fox_attention(u16, kx, vt, tq):
    b, s, _ = u16.shape
    pairs = GROUP // LANES
    return pl.pallas_call(
        functools.partial(_fox_kernel, tq=tq),
        out_shape=jax.ShapeDtypeStruct((b, s, GROUP), BF16),
        grid=(b, pairs, s // tq),
        in_specs=[pl.BlockSpec((1, tq, LANES), lambda bi, hp, i: (bi, i, hp)),
                  pl.BlockSpec((1, s, 2 * LANES), lambda bi, hp, i: (bi, 0, hp)),
                  pl.BlockSpec((1, LANES, s), lambda bi, hp, i: (bi, hp, 0))],
        out_specs=pl.BlockSpec((1, tq, LANES), lambda bi, hp, i: (bi, i, hp)),
        scratch_shapes=[pltpu.VMEM((2, 1, tq), F32), pltpu.VMEM((2, 1, tq), F32),
                        pltpu.VMEM((LANES, tq), F32)],
        compiler_params=_cparams(("parallel", "parallel", "arbitrary")),
        name="fox_attention",
    )(u16, kx, vt)


def _pool_kernel(u_ref, halo_ref, w_ref, scale_ref, o_ref, *, tm):
    i = pl.program_id(1)
    x = u_ref[0]
    halo = jnp.where(i > 0, halo_ref[0], 0.0)
    e = jnp.concatenate([halo, x], axis=0)
    sums = []
    acc = e
    for k in (1, 2, 4, 8):
        acc = acc + pltpu.roll(acc, k, 0)
        sums.append(acc[POOL_HALO:, :])
    lane = lax.broadcasted_iota(jnp.int32, (1, GROUP), 1)
    grp = lane // HEAD_DIM
    wsum = jnp.where(grp == 0, sums[0], jnp.where(grp == 1, sums[1], jnp.where(grp == 2, sums[2], sums[3])))
    width = jnp.where(grp == 0, POOL_WINDOWS[0],
                      jnp.where(grp == 1, POOL_WINDOWS[1],
                                jnp.where(grp == 2, POOL_WINDOWS[2], POOL_WINDOWS[3])))
    pos = i * tm + lax.broadcasted_iota(jnp.int32, (tm, 1), 0)
    count = jnp.minimum(pos + 1, width).astype(F32)
    delta = (wsum / count - x).astype(BF16)
    o_ref[0] = (_dot(delta, w_ref[...]) * scale_ref[...]).astype(BF16)


def _pool_mixer(u32, w_bd, scale, tm):
    b, s, _ = u32.shape
    hb = tm // POOL_HALO
    return pl.pallas_call(
        functools.partial(_pool_kernel, tm=tm),
        out_shape=jax.ShapeDtypeStruct((b, s, GROUP), BF16),
        grid=(b, s // tm),
        in_specs=[pl.BlockSpec((1, tm, GROUP), lambda bi, i: (bi, i, COL_POOL)),
                  pl.BlockSpec((1, POOL_HALO, GROUP), lambda bi, i: (bi, jnp.maximum(i * hb - 1, 0), COL_POOL)),
                  pl.BlockSpec((GROUP, GROUP), lambda bi, i: (0, 0)),
                  pl.BlockSpec((1, GROUP), lambda bi, i: (0, 0))],
        out_specs=pl.BlockSpec((1, tm, GROUP), lambda bi, i: (bi, i, 0)),
        compiler_params=_cparams(("parallel", "parallel")),
        name="pool_mixer",
    )(u32, u32, w_bd, scale)


def _conv_kernel(a_ref, g_ref, ha_ref, hg_ref, w_ref, b_ref, lng_ref, lnb_ref, o_ref, h_ref, *, tm, rows):
    i = pl.program_id(1)
    h_ref[0:CONV_HALO, :] = jnp.where(i > 0, ha_ref[0] * jax.nn.sigmoid(hg_ref[0]), 0.0)
    h_ref[CONV_HALO:, :] = a_ref[0] * jax.nn.sigmoid(g_ref[0])
    first = CONV_HALO - (CONV_WIDTH - 1)
    for r0 in range(0, tm, rows):
        acc = jnp.zeros((rows, GROUP), F32) + b_ref[...]
        for j in range(CONV_WIDTH):
            acc = acc + w_ref[j:j + 1, :] * h_ref[r0 + first + j:r0 + first + j + rows, :]
        mu = jnp.mean(acc, axis=-1, keepdims=True)
        d = acc - mu
        var = jnp.mean(d * d, axis=-1, keepdims=True)
        y = d * lax.rsqrt(var + EPS) * lng_ref[...] + lnb_ref[...]
        o_ref[0, r0:r0 + rows, :] = _silu(y).astype(BF16)


def _conv_mixer(u32, w, bias, ln_g, ln_b, tm):
    b, s, _ = u32.shape
    hb = tm // CONV_HALO
    rows = min(tm, 64)
    row = lambda bi, i: (0, 0)
    halo = lambda col: (lambda bi, i: (bi, jnp.maximum(i * hb - 1, 0), col))
    return pl.pallas_call(
        functools.partial(_conv_kernel, tm=tm, rows=rows),
        out_shape=jax.ShapeDtypeStruct((b, s, GROUP), BF16),
        grid=(b, s // tm),
        in_specs=[pl.BlockSpec((1, tm, GROUP), lambda bi, i: (bi, i, COL_CA)),
                  pl.BlockSpec((1, tm, GROUP), lambda bi, i: (bi, i, COL_CG)),
                  pl.BlockSpec((1, CONV_HALO, GROUP), halo(COL_CA)),
                  pl.BlockSpec((1, CONV_HALO, GROUP), halo(COL_CG)),
                  pl.BlockSpec((CONV_HALO, GROUP), row),
                  pl.BlockSpec((1, GROUP), row), pl.BlockSpec((1, GROUP), row), pl.BlockSpec((1, GROUP), row)],
        out_specs=pl.BlockSpec((1, tm, GROUP), lambda bi, i: (bi, i, 0)),
        scratch_shapes=[pltpu.VMEM((CONV_HALO + tm, GROUP), F32)],
        compiler_params=_cparams(("parallel", "parallel")),
        name="conv_mixer",
    )(u32, u32, u32, u32, w, bias, ln_g, ln_b)


def _ret_kernel(q_ref, k_ref, v_ref, g_ref, cos_ref, sin_ref, gn_ref, o_ref, state_ref, *, tr):
    C = RET_CHUNK

    @pl.when(pl.program_id(1) == 0)
    def _():
        state_ref[...] = jnp.zeros(state_ref.shape, F32)

    lane = lax.broadcasted_iota(jnp.int32, (1, LANES), 1)
    lo = lane < HEAD_DIM
    first_half = (lane % HEAD_DIM) < (HEAD_DIM // 2)
    ri = lax.broadcasted_iota(jnp.int32, (C, C), 0)
    ci = lax.broadcasted_iota(jnp.int32, (C, C), 1)
    diff = (ri - ci).astype(F32)
    tok = lax.broadcasted_iota(jnp.int32, (C, 1), 0).astype(F32)
    same_head = (ri < HEAD_DIM) == (ci < HEAD_DIM)
    seg_mean = jnp.where(same_head, 1.0 / HEAD_DIM, 0.0).astype(BF16)

    def seg_mean_dot(x):
        hi = x.astype(BF16)
        lo_part = (x - hi.astype(F32)).astype(BF16)
        return _dot(hi, seg_mean) + _dot(lo_part, seg_mean)

    cos = cos_ref[0]
    sin = sin_ref[0]

    def rope(x):
        swapped = jnp.where(first_half, pltpu.roll(x, LANES - HEAD_DIM // 2, 1), pltpu.roll(x, HEAD_DIM // 2, 1))
        return x * cos + swapped * sin

    for pr in range(RET_HEADS // 2):
        lg = [math.log(1.0 - 2.0 ** (-5.0 - (2 * pr + hh))) for hh in range(2)]
        dmask = [jnp.where(diff >= 0, jnp.exp(lg[hh] * jnp.maximum(diff, 0.0)), 0.0) for hh in range(2)]
        zeta = jnp.where(lo, jnp.exp(lg[0] * (C - 1 - tok)), jnp.exp(lg[1] * (C - 1 - tok)))
        xi = jnp.where(lo, jnp.exp(lg[0] * (tok + 1)), jnp.exp(lg[1] * (tok + 1)))
        row_is_lo = lax.broadcasted_iota(jnp.int32, (LANES, 1), 0) < HEAD_DIM
        decay = jnp.where(row_is_lo, math.exp(lg[0] * C), math.exp(lg[1] * C))
        cols = slice(pr * LANES, (pr + 1) * LANES)
        qr = rope(q_ref[0, :, cols]).astype(BF16)
        kf = rope(k_ref[0, :, cols]) * (HEAD_DIM ** -0.5)
        state = state_ref[pr]
        for n in range(tr // C):
            rows = slice(n * C, (n + 1) * C)
            qc = qr[rows]
            kc = kf[rows]
            kb = kc.astype(BF16)
            vb = v_ref[0, rows, cols].astype(BF16)
            zero = jnp.zeros_like(qc)
            intra = []
            for hh in range(2):
                qh = jnp.where(lo, qc, zero) if hh == 0 else jnp.where(lo, zero, qc)
                sc = (_dot_nt(qh, kb) * dmask[hh]).astype(BF16)
                intra.append(_dot(sc, vb))
            cross = _dot(qc, state.astype(BF16)) * xi
            o = jnp.where(lo, intra[0], intra[1]) + cross
            kv = _dot_tn((kc * zeta).astype(BF16), vb)
            state = state * decay + jnp.where(same_head, kv, 0.0)
            mu = seg_mean_dot(o)
            d = o - mu
            var = seg_mean_dot(d * d)
            y = d * lax.rsqrt(var + EPS) * gn_ref[:, cols]
            o_ref[0, rows, cols] = (_silu(g_ref[0, rows, cols]) * y).astype(BF16)
        state_ref[pr] = state


def _retention(u32, cos_t, sin_t, gn_g, tr):
    b, s, _ = u32.shape
    blk = lambda col: pl.BlockSpec((1, tr, GROUP), lambda bi, i: (bi, i, col))
    tab = pl.BlockSpec((1, tr, LANES), lambda bi, i: (bi, i, 0))
    return pl.pallas_call(
        functools.partial(_ret_kernel, tr=tr),
        out_shape=jax.ShapeDtypeStruct((b, s, GROUP), BF16),
        grid=(b, s // tr),
        in_specs=[blk(COL_RQ), blk(COL_RK), blk(COL_RV), blk(COL_RG), tab, tab,
                  pl.BlockSpec((1, GROUP), lambda bi, i: (0, 0))],
        out_specs=pl.BlockSpec((1, tr, GROUP), lambda bi, i: (bi, i, 0)),
        scratch_shapes=[pltpu.VMEM((RET_HEADS // 2, LANES, LANES), F32)],
        compiler_params=_cparams(("parallel", "arbitrary")),
        name="retention",
    )(u32, u32, u32, u32, cos_t, sin_t, gn_g)


def _dense_kernel(x_ref, yf_ref, yp_ref, yr_ref, yc_ref, mod_ref, g_ref, fg_ref,
                  wo_ref, w1_ref, w3_ref, w2_ref, o_ref, *, ff_chunk, final):
    mix = _dot(yf_ref[0], wo_ref[0:GROUP, :])
    for n, y_ref in enumerate((yp_ref, yr_ref, yc_ref), start=1):
        mix = mix + _dot(y_ref[0], wo_ref[n * GROUP:(n + 1) * GROUP, :])
    x = x_ref[0] + mod_ref[0, 2:3, :] * mix
    h = _rms_mod(x, g_ref[...], mod_ref[0, 4:5, :], mod_ref[0, 3:4, :]).astype(BF16)
    d_ff = w1_ref.shape[1]
    f = None
    for c in range(0, d_ff, ff_chunk):
        a = _dot(h, w1_ref[:, c:c + ff_chunk])
        gated = (_silu(a) * _dot(h, w3_ref[:, c:c + ff_chunk])).astype(BF16)
        part = _dot(gated, w2_ref[c:c + ff_chunk, :])
        f = part if f is None else f + part
    x = x + mod_ref[0, 5:6, :] * f
    if final:
        ms = jnp.mean(x * x, axis=-1, keepdims=True)
        x = x * lax.rsqrt(ms + EPS) * fg_ref[...]
    o_ref[0] = x


def _dense(x, ys, mod_l, g, final_g, wo, w1, w3, w2, tm, final):
    b, s, d = x.shape
    d_ff = w1.shape[1]
    ff_chunk = 256
    assert d_ff % ff_chunk == 0
    const = lambda bi, i: (0, 0)
    resident = lambda shape: pl.BlockSpec(shape, const, pipeline_mode=pl.Buffered(1))
    tile = lambda w: pl.BlockSpec((1, tm, w), lambda bi, i: (bi, i, 0))
    return pl.pallas_call(
        functools.partial(_dense_kernel, ff_chunk=ff_chunk, final=final),
        out_shape=jax.ShapeDtypeStruct((b, s, d), F32),
        grid=(b, s // tm),
        in_specs=[tile(d), tile(GROUP), tile(GROUP), tile(GROUP), tile(GROUP),
                  pl.BlockSpec((1, 6, d), lambda bi, i: (bi, 0, 0)),
                  pl.BlockSpec((1, d), const), pl.BlockSpec((1, d), const),
                  resident((d, d)), resident((d, d_ff)), resident((d, d_ff)), resident((d_ff, d))],
        out_specs=tile(d),
        compiler_params=_cparams(("parallel", "parallel")),
        name="out_proj_swiglu",
    )(x, *ys, mod_l, g, final_g, wo, w1, w3, w2)


def kernel(x, c, positions, ada_w, ada_b, norm_mix_g, norm_ffn_g, w_in, fox_fb, pool_w, pool_scale,
           ret_gn_g, conv_w, conv_b, conv_ln_g, conv_ln_b, w_out, ffn_w1, ffn_w3, ffn_w2, final_g):
    b, s, d = x.shape
    depth = ada_w.shape[0]
    tm = min(s, 512)
    n_fox = 3 * GROUP
    n_ff = fox_fb.shape[1]

    mod = _ada_mod(c, ada_w, ada_b).reshape(depth, b, 6, d)
    cos_t, sin_t = _rope_tables(positions)

    for l in range(depth):
        w16 = w_in[l][:, :2 * GROUP].astype(BF16)
        wvt = w_in[l][:, 2 * GROUP:n_fox].T.astype(BF16)
        w32 = jnp.concatenate(
            [w_in[l][:, n_fox + n_ff:], w_in[l][:, n_fox:n_fox + n_ff],
             jnp.zeros((d, LANES - n_ff), F32)], axis=1).astype(BF16)
        assert w32.shape[1] == U32_WIDTH
        u16, vt, u32 = _inproj(x, mod[l], norm_mix_g[l].reshape(1, d), w16, wvt, w32, tm)

        fb_row = jnp.pad(fox_fb[l], (0, LANES - n_ff)).reshape(1, LANES)
        kx = _forget_keys(u32, u16, fb_row)
        y_fox = _fox_attention(u16, kx, vt, tm)

        pool_bd = jax.scipy.linalg.block_diag(*[pool_w[l][gi] for gi in range(len(POOL_WINDOWS))]).astype(BF16)
        y_pool = _pool_mixer(u32, pool_bd, pool_scale[l].reshape(1, GROUP), tm)

        y_ret = _retention(u32, cos_t, sin_t, ret_gn_g[l].reshape(1, GROUP), tm)

        conv_w_pad = jnp.pad(conv_w[l], ((0, CONV_HALO - CONV_WIDTH), (0, 0)))
        y_conv = _conv_mixer(u32, conv_w_pad, conv_b[l].reshape(1, GROUP), conv_ln_g[l].reshape(1, GROUP),
                             conv_ln_b[l].reshape(1, GROUP), tm)

        x = _dense(x, (y_fox, y_pool, y_ret, y_conv), mod[l], norm_ffn_g[l].reshape(1, d),
                   final_g.reshape(1, d), w_out[l].astype(BF16), ffn_w1[l].astype(BF16),
                   ffn_w3[l].astype(BF16), ffn_w2[l].astype(BF16), tm, final=(l == depth - 1))
    return x
```

```python
import functools
import math

import jax
import jax.numpy as jnp
from jax import lax
from jax.experimental import pallas as pl
from jax.experimental.pallas import tpu as pltpu

F32 = jnp.float32
BF16 = jnp.bfloat16

EPS = 1e-6
HEAD_DIM = 64
LANES = 128
SUBLANES = 8
LOG2E = math.log2(math.e)
GROUP = 256
POOL_WINDOWS = (2, 4, 8, 16)
POOL_HALO = 16
CONV_WIDTH = 31
CONV_HALO = 32
RET_CHUNK = 128
RET_HEADS = 4
ROPE_BASE = 10000.0
VMEM_LIMIT = 56 * 1024 * 1024

U32_WIDTH = 1920
COL_POOL, COL_RQ, COL_RK, COL_RV, COL_RG, COL_CA, COL_CG = 0, 1, 2, 3, 4, 5, 6
COL_FF_128 = 14


def _cparams(sem):
    return pltpu.CompilerParams(dimension_semantics=sem, vmem_limit_bytes=VMEM_LIMIT)


def _silu(x):
    return x * jax.nn.sigmoid(x)


def _rms_mod(x, g, scale, shift):
    ms = jnp.mean(x * x, axis=-1, keepdims=True)
    return (x * lax.rsqrt(ms + EPS) * g) * (1.0 + scale) + shift


def _dot(a, b):
    return jnp.dot(a, b, preferred_element_type=F32)


def _dot_nt(a, b):
    return lax.dot_general(a, b, (((1,), (1,)), ((), ())), preferred_element_type=F32)


def _dot_tn(a, b):
    return lax.dot_general(a, b, (((0,), (0,)), ((), ())), preferred_element_type=F32)


def _ada_kernel(c_ref, w_ref, b_ref, o_ref):
    ca = _silu(c_ref[...]).astype(BF16)
    o_ref[0] = _dot(ca, w_ref[0].astype(BF16)) + b_ref[0]


def _ada_mod(c, ada_w, ada_b):
    depth, d, n = ada_w.shape
    b = c.shape[0]
    tn = 1536
    return pl.pallas_call(
        _ada_kernel,
        out_shape=jax.ShapeDtypeStruct((depth, b, n), F32),
        grid=(depth, n // tn),
        in_specs=[pl.BlockSpec((b, d), lambda l, j: (0, 0)),
                  pl.BlockSpec((1, d, tn), lambda l, j: (l, 0, j)),
                  pl.BlockSpec((1, 1, tn), lambda l, j: (l, 0, j))],
        out_specs=pl.BlockSpec((1, b, tn), lambda l, j: (l, 0, j)),
        compiler_params=_cparams(("parallel", "parallel")),
        name="ada_mod",
    )(c, ada_w, ada_b.reshape(depth, 1, n))


def _rope_kernel(pos_ref, inv_ref, cos_ref, sin_ref):
    ang = pos_ref[0].astype(F32) * inv_ref[...]
    lane = lax.broadcasted_iota(jnp.int32, (1, LANES), 1)
    first_half = (lane % HEAD_DIM) < (HEAD_DIM // 2)
    cos_ref[0] = jnp.cos(ang)
    s = jnp.sin(ang)
    sin_ref[0] = jnp.where(first_half, -s, s)


def _rope_tables(positions):
    b, s = positions.shape
    half = HEAD_DIM // 2
    inv = ROPE_BASE ** (-jnp.arange(half, dtype=F32) / half)
    inv = jnp.tile(inv, LANES // half).reshape(1, LANES)
    ts = min(s, 512)
    return pl.pallas_call(
        _rope_kernel,
        out_shape=(jax.ShapeDtypeStruct((b, s, LANES), F32),) * 2,
        grid=(b, s // ts),
        in_specs=[pl.BlockSpec((1, ts, 1), lambda bi, i: (bi, i, 0)),
                  pl.BlockSpec((1, LANES), lambda bi, i: (0, 0))],
        out_specs=(pl.BlockSpec((1, ts, LANES), lambda bi, i: (bi, i, 0)),) * 2,
        compiler_params=_cparams(("parallel", "parallel")),
        name="rope_tables",
    )(positions.reshape(b, s, 1), inv)


def _inproj_kernel(x_ref, mod_ref, g_ref, w16_ref, wvt_ref, w32_ref, u16_ref, vt_ref, u32_ref):
    h = _rms_mod(x_ref[0], g_ref[...], mod_ref[0, 1:2, :], mod_ref[0, 0:1, :]).astype(BF16)
    n16 = u16_ref.shape[-1]
    for c in range(0, n16, 256):
        u16_ref[0, :, c:c + 256] = _dot(h, w16_ref[:, c:c + 256]).astype(BF16)
    vt_ref[0] = _dot_nt(wvt_ref[...], h).astype(BF16)
    n32 = u32_ref.shape[-1]
    for c in range(0, n32, 384):
        u32_ref[0, :, c:c + 384] = _dot(h, w32_ref[:, c:c + 384])


def _inproj(x, mod_l, g, w16, wvt, w32, tm):
    b, s, d = x.shape
    n16, n32 = w16.shape[1], w32.shape[1]
    const = lambda bi, i: (0, 0)
    return pl.pallas_call(
        _inproj_kernel,
        out_shape=(jax.ShapeDtypeStruct((b, s, n16), BF16), jax.ShapeDtypeStruct((b, GROUP, s), BF16),
                   jax.ShapeDtypeStruct((b, s, n32), F32)),
        grid=(b, s // tm),
        in_specs=[pl.BlockSpec((1, tm, d), lambda bi, i: (bi, i, 0)),
                  pl.BlockSpec((1, 6, d), lambda bi, i: (bi, 0, 0)),
                  pl.BlockSpec((1, d), const),
                  pl.BlockSpec((d, n16), const),
                  pl.BlockSpec((GROUP, d), const),
                  pl.BlockSpec((d, n32), const)],
        out_specs=(pl.BlockSpec((1, tm, n16), lambda bi, i: (bi, i, 0)),
                   pl.BlockSpec((1, GROUP, tm), lambda bi, i: (bi, 0, i)),
                   pl.BlockSpec((1, tm, n32), lambda bi, i: (bi, i, 0))),
        compiler_params=_cparams(("parallel", "parallel")),
        name="in_proj",
    )(x, mod_l, g, w16, wvt, w32)


BIAS_TERMS = 3
FOX_ACC_ROWS = HEAD_DIM + 16


def _cum_kernel(ff_ref, fb_ref, k_ref, kx_ref, *, blk):
    s = ff_ref.shape[1]
    r = lax.broadcasted_iota(jnp.int32, (blk, blk), 0)
    c = lax.broadcasted_iota(jnp.int32, (blk, blk), 1)
    tri = (c <= r).astype(F32)
    pr = lax.broadcasted_iota(jnp.int32, (LANES, LANES), 0)
    pc = lax.broadcasted_iota(jnp.int32, (LANES, LANES), 1)

    def place(pair, term):
        hit = ((pr == 2 * pair) & (pc == term)) | ((pr == 2 * pair + 1) & (pc == BIAS_TERMS + term))
        return jnp.where(hit, 1.0, 0.0).astype(BF16)

    def body(i, carry):
        r0 = pl.multiple_of(i * blk, blk)
        z = ff_ref[0, pl.ds(r0, blk), :] + fb_ref[...]
        lf = jnp.minimum(z, 0.0) - jnp.log1p(jnp.exp(-jnp.abs(z)))
        cum = jnp.dot(tri, lf, preferred_element_type=F32, precision=lax.Precision.HIGHEST) + carry
        terms = []
        rest = cum * (-LOG2E)
        for _ in range(BIAS_TERMS):
            t = rest.astype(BF16)
            terms.append(t)
            rest = rest - t.astype(F32)
        for pair in range(GROUP // LANES):
            bias = _dot(terms[0], place(pair, 0))
            for t in range(1, BIAS_TERMS):
                bias = bias + _dot(terms[t], place(pair, t))
            base = 2 * pair * LANES
            kx_ref[0, pl.ds(r0, blk), base:base + LANES] = k_ref[0, pl.ds(r0, blk), pair * LANES:(pair + 1) * LANES]
            kx_ref[0, pl.ds(r0, blk), base + LANES:base + 2 * LANES] = bias.astype(BF16)
        return cum[blk - 1:blk, :]

    lax.fori_loop(0, s // blk, body, jnp.zeros((1, LANES), F32))


def _forget_keys(u32, u16, fb_row):
    b, s, _ = u32.shape
    blk = min(s, 256)
    return pl.pallas_call(
        functools.partial(_cum_kernel, blk=blk),
        out_shape=jax.ShapeDtypeStruct((b, s, 2 * GROUP), BF16),
        grid=(b,),
        in_specs=[pl.BlockSpec((1, s, LANES), lambda bi: (bi, 0, COL_FF_128)),
                  pl.BlockSpec((1, LANES), lambda bi: (0, 0)),
                  pl.BlockSpec((1, s, GROUP), lambda bi: (bi, 0, 1))],
        out_specs=pl.BlockSpec((1, s, 2 * GROUP), lambda bi: (bi, 0, 0)),
        compiler_params=_cparams(("parallel",)),
        name="forget_keys",
    )(u32, fb_row, u16)


def _fox_kernel(q_ref, kx_ref, vt_ref, o_ref, m_ref, acc_ref, *, tq):
    i = pl.program_id(2)
    lane = lax.broadcasted_iota(jnp.int32, (1, LANES), 1)
    lo = lane < HEAD_DIM
    q = (q_ref[0].astype(F32) * (HEAD_DIM ** -0.5 * LOG2E)).astype(BF16)
    qx = []
    for hh in range(2):
        own = jnp.where(lo == (hh == 0), 1.0, 0.0).astype(BF16)
        picks = jnp.where(lane // BIAS_TERMS == hh, 1.0, 0.0).astype(BF16)
        qx.append(jnp.concatenate([q * own, jnp.broadcast_to(picks, q.shape)], axis=1))
    m_ref[...] = jnp.full(m_ref.shape, -jnp.inf, F32)
    acc_ref[...] = jnp.zeros(acc_ref.shape, F32)
    key = lax.broadcasted_iota(jnp.int32, (tq, tq), 0)
    qry = lax.broadcasted_iota(jnp.int32, (tq, tq), 1)
    ones_rows = jnp.ones((FOX_ACC_ROWS - HEAD_DIM, tq), BF16)

    def step(blocks):
        starts = [pl.multiple_of(j * tq, tq) for j, _ in blocks]
        scores = []
        for c0, (_, diagonal) in zip(starts, blocks):
            kx = kx_ref[0, pl.ds(c0, tq), :]
            per_head = [_dot_nt(kx, qx[hh]) for hh in range(2)]
            if diagonal:
                per_head = [jnp.where(key <= qry, s, -jnp.inf) for s in per_head]
            scores.append(per_head)
        probs, alphas = [], []
        for hh in range(2):
            m_old = m_ref[hh]
            m_new = m_old
            for per_head in scores:
                m_new = jnp.maximum(m_new, jnp.max(per_head[hh], axis=0, keepdims=True))
            alphas.append(jnp.exp2(m_old - m_new))
            probs.append([jnp.exp2(per_head[hh] - m_new).astype(BF16) for per_head in scores])
            m_ref[hh] = m_new
        for hh in range(2):
            pv = None
            for c0, p in zip(starts, probs[hh]):
                vt = vt_ref[0, hh * HEAD_DIM:(hh + 1) * HEAD_DIM, pl.ds(c0, tq)]
                part = _dot(jnp.concatenate([vt, ones_rows], axis=0), p)
                pv = part if pv is None else pv + part
            acc_ref[hh] = acc_ref[hh] * alphas[hh] + pv

    def off_diagonal_pair(t, carry):
        step([(2 * t, False), (2 * t + 1, False)])
        return carry

    lax.fori_loop(0, i // 2, off_diagonal_pair, 0)

    @pl.when(i % 2 == 1)
    def _():
        step([(i - 1, False), (i, True)])

    @pl.when(i % 2 == 0)
    def _():
        step([(i, True)])

    out_t = jnp.concatenate(
        [acc_ref[hh, 0:HEAD_DIM, :] / acc_ref[hh, HEAD_DIM:HEAD_DIM + 1, :] for hh in range(2)], axis=0)
    o_ref[0] = out_t.T.astype(BF16)


def _fox_attention(u16, kx, vt, tq):
    b, s, _ = u16.shape
    pairs = GROUP // LANES
    return pl.pallas_call(
        functools.partial(_fox_kernel, tq=tq),
        out_shape=jax.ShapeDtypeStruct((b, s, GROUP), BF16),
        grid=(b, pairs, s // tq),
        in_specs=[pl.BlockSpec((1, tq, LANES), lambda bi, hp, i: (bi, i, hp)),
                  pl.BlockSpec((1, s, 2 * LANES), lambda bi, hp, i: (bi, 0, hp)),
                  pl.BlockSpec((1, LANES, s), lambda bi, hp, i: (bi, hp, 0))],
        out_specs=pl.BlockSpec((1, tq, LANES), lambda bi, hp, i: (bi, i, hp)),
        scratch_shapes=[pltpu.VMEM((2, 1, tq), F32), pltpu.VMEM((2, FOX_ACC_ROWS, tq), F32)],
        compiler_params=_cparams(("parallel", "parallel", "arbitrary")),
        name="fox_attention",
    )(u16, kx, vt)


def _pool_kernel(u_ref, halo_ref, w_ref, scale_ref, o_ref, *, tm):
    i = pl.program_id(1)
    x = u_ref[0]
    halo = jnp.where(i > 0, halo_ref[0], 0.0)
    e = jnp.concatenate([halo, x], axis=0)
    sums = []
    acc = e
    for k in (1, 2, 4, 8):
        acc = acc + pltpu.roll(acc, k, 0)
        sums.append(acc[POOL_HALO:, :])
    lane = lax.broadcasted_iota(jnp.int32, (1, GROUP), 1)
    grp = lane // HEAD_DIM
    wsum = jnp.where(grp == 0, sums[0], jnp.where(grp == 1, sums[1], jnp.where(grp == 2, sums[2], sums[3])))
    width = jnp.where(grp == 0, POOL_WINDOWS[0],
                      jnp.where(grp == 1, POOL_WINDOWS[1],
                                jnp.where(grp == 2, POOL_WINDOWS[2], POOL_WINDOWS[3])))
    pos = i * tm + lax.broadcasted_iota(jnp.int32, (tm, 1), 0)
    count = jnp.minimum(pos + 1, width).astype(F32)
    delta = (wsum / count - x).astype(BF16)
    o_ref[0] = (_dot(delta, w_ref[...]) * scale_ref[...]).astype(BF16)


def _pool_mixer(u32, w_bd, scale, tm):
    b, s, _ = u32.shape
    hb = tm // POOL_HALO
    return pl.pallas_call(
        functools.partial(_pool_kernel, tm=tm),
        out_shape=jax.ShapeDtypeStruct((b, s, GROUP), BF16),
        grid=(b, s // tm),
        in_specs=[pl.BlockSpec((1, tm, GROUP), lambda bi, i: (bi, i, COL_POOL)),
                  pl.BlockSpec((1, POOL_HALO, GROUP), lambda bi, i: (bi, jnp.maximum(i * hb - 1, 0), COL_POOL)),
                  pl.BlockSpec((GROUP, GROUP), lambda bi, i: (0, 0)),
                  pl.BlockSpec((1, GROUP), lambda bi, i: (0, 0))],
        out_specs=pl.BlockSpec((1, tm, GROUP), lambda bi, i: (bi, i, 0)),
        compiler_params=_cparams(("parallel", "parallel")),
        name="pool_mixer",
    )(u32, u32, w_bd, scale)


def _conv_kernel(a_ref, g_ref, ha_ref, hg_ref, w_ref, b_ref, lng_ref, lnb_ref, o_ref, h_ref, hs_ref, *, tm, rows):
    i = pl.program_id(1)
    h_ref[0:CONV_HALO, :] = jnp.where(i > 0, ha_ref[0] * jax.nn.sigmoid(hg_ref[0]), 0.0)
    h_ref[CONV_HALO:, :] = a_ref[0] * jax.nn.sigmoid(g_ref[0])
    n = hs_ref.shape[1]
    for r in range(1, SUBLANES):
        hs_ref[r - 1] = h_ref[r:r + n, :]
    first = CONV_HALO - (CONV_WIDTH - 1)
    for r0 in range(0, tm, rows):
        acc = jnp.zeros((rows, GROUP), F32) + b_ref[...]
        for j in range(CONV_WIDTH):
            shift = (first + j) % SUBLANES
            base = r0 + first + j - shift
            taps = h_ref[base:base + rows, :] if shift == 0 else hs_ref[shift - 1, base:base + rows, :]
            acc = acc + w_ref[j:j + 1, :] * taps
        mu = jnp.mean(acc, axis=-1, keepdims=True)
        d = acc - mu
        var = jnp.mean(d * d, axis=-1, keepdims=True)
        y = d * lax.rsqrt(var + EPS) * lng_ref[...] + lnb_ref[...]
        o_ref[0, r0:r0 + rows, :] = _silu(y).astype(BF16)


def _conv_mixer(u32, w, bias, ln_g, ln_b, tm):
    b, s, _ = u32.shape
    hb = tm // CONV_HALO
    rows = min(tm, 64)
    row = lambda bi, i: (0, 0)
    halo = lambda col: (lambda bi, i: (bi, jnp.maximum(i * hb - 1, 0), col))
    return pl.pallas_call(
        functools.partial(_conv_kernel, tm=tm, rows=rows),
        out_shape=jax.ShapeDtypeStruct((b, s, GROUP), BF16),
        grid=(b, s // tm),
        in_specs=[pl.BlockSpec((1, tm, GROUP), lambda bi, i: (bi, i, COL_CA)),
                  pl.BlockSpec((1, tm, GROUP), lambda bi, i: (bi, i, COL_CG)),
                  pl.BlockSpec((1, CONV_HALO, GROUP), halo(COL_CA)),
                  pl.BlockSpec((1, CONV_HALO, GROUP), halo(COL_CG)),
                  pl.BlockSpec((CONV_HALO, GROUP), row),
                  pl.BlockSpec((1, GROUP), row), pl.BlockSpec((1, GROUP), row), pl.BlockSpec((1, GROUP), row)],
        out_specs=pl.BlockSpec((1, tm, GROUP), lambda bi, i: (bi, i, 0)),
        scratch_shapes=[pltpu.VMEM((CONV_HALO + tm, GROUP), F32),
                        pltpu.VMEM((SUBLANES - 1, CONV_HALO + tm - SUBLANES, GROUP), F32)],
        compiler_params=_cparams(("parallel", "parallel")),
        name="conv_mixer",
    )(u32, u32, u32, u32, w, bias, ln_g, ln_b)


def _ret_kernel(q_ref, k_ref, v_ref, g_ref, cos_ref, sin_ref, gn_ref, o_ref, state_ref, raw_ref, *, tr):
    C = RET_CHUNK

    @pl.when(pl.program_id(1) == 0)
    def _():
        state_ref[...] = jnp.zeros(state_ref.shape, F32)

    lane = lax.broadcasted_iota(jnp.int32, (1, LANES), 1)
    lo = lane < HEAD_DIM
    first_half = (lane % HEAD_DIM) < (HEAD_DIM // 2)
    ri = lax.broadcasted_iota(jnp.int32, (C, C), 0)
    ci = lax.broadcasted_iota(jnp.int32, (C, C), 1)
    diff = (ri - ci).astype(F32)
    tok = lax.broadcasted_iota(jnp.int32, (C, 1), 0).astype(F32)
    same_head = (ri < HEAD_DIM) == (ci < HEAD_DIM)
    gi = lax.broadcasted_iota(jnp.int32, (GROUP, GROUP), 0) // HEAD_DIM
    gj = lax.broadcasted_iota(jnp.int32, (GROUP, GROUP), 1) // HEAD_DIM
    seg_mean = jnp.where(gi == gj, 1.0 / HEAD_DIM, 0.0).astype(BF16)

    def seg_mean_dot(x):
        hi = x.astype(BF16)
        lo_part = (x - hi.astype(F32)).astype(BF16)
        return _dot(hi, seg_mean) + _dot(lo_part, seg_mean)

    cos = cos_ref[0]
    sin = sin_ref[0]

    def rope(x):
        swapped = jnp.where(first_half, pltpu.roll(x, LANES - HEAD_DIM // 2, 1), pltpu.roll(x, HEAD_DIM // 2, 1))
        return x * cos + swapped * sin

    for pr in range(RET_HEADS // 2):
        lg = [math.log(1.0 - 2.0 ** (-5.0 - (2 * pr + hh))) for hh in range(2)]
        dmask = [jnp.where(diff >= 0, jnp.exp(lg[hh] * jnp.maximum(diff, 0.0)), 0.0) for hh in range(2)]
        zeta = jnp.where(lo, jnp.exp(lg[0] * (C - 1 - tok)), jnp.exp(lg[1] * (C - 1 - tok)))
        xi = jnp.where(lo, jnp.exp(lg[0] * (tok + 1)), jnp.exp(lg[1] * (tok + 1)))
        row_is_lo = lax.broadcasted_iota(jnp.int32, (LANES, 1), 0) < HEAD_DIM
        decay = jnp.where(row_is_lo, math.exp(lg[0] * C), math.exp(lg[1] * C))
        cols = slice(pr * LANES, (pr + 1) * LANES)
        qr = rope(q_ref[0, :, cols]).astype(BF16)
        kf = rope(k_ref[0, :, cols]) * (HEAD_DIM ** -0.5)
        state = state_ref[pr]
        for n in range(tr // C):
            rows = slice(n * C, (n + 1) * C)
            qc = qr[rows]
            kc = kf[rows]
            kb = kc.astype(BF16)
            vb = v_ref[0, rows, cols].astype(BF16)
            zero = jnp.zeros_like(qc)
            intra = []
            for hh in range(2):
                qh = jnp.where(lo, qc, zero) if hh == 0 else jnp.where(lo, zero, qc)
                sc = (_dot_nt(qh, kb) * dmask[hh]).astype(BF16)
                intra.append(_dot(sc, vb))
            cross = _dot(qc, state.astype(BF16)) * xi
            raw_ref[rows, cols] = jnp.where(lo, intra[0], intra[1]) + cross
            kv = _dot_tn((kc * zeta).astype(BF16), vb)
            state = state * decay + jnp.where(same_head, kv, 0.0)
        state_ref[pr] = state

    o = raw_ref[...]
    d = o - seg_mean_dot(o)
    var = seg_mean_dot(d * d)
    y = d * lax.rsqrt(var + EPS) * gn_ref[...]
    o_ref[0] = (_silu(g_ref[0]) * y).astype(BF16)


def _retention(u32, cos_t, sin_t, gn_g, tr):
    b, s, _ = u32.shape
    blk = lambda col: pl.BlockSpec((1, tr, GROUP), lambda bi, i: (bi, i, col))
    tab = pl.BlockSpec((1, tr, LANES), lambda bi, i: (bi, i, 0))
    return pl.pallas_call(
        functools.partial(_ret_kernel, tr=tr),
        out_shape=jax.ShapeDtypeStruct((b, s, GROUP), BF16),
        grid=(b, s // tr),
        in_specs=[blk(COL_RQ), blk(COL_RK), blk(COL_RV), blk(COL_RG), tab, tab,
                  pl.BlockSpec((1, GROUP), lambda bi, i: (0, 0))],
        out_specs=pl.BlockSpec((1, tr, GROUP), lambda bi, i: (bi, i, 0)),
        scratch_shapes=[pltpu.VMEM((RET_HEADS // 2, LANES, LANES), F32), pltpu.VMEM((tr, GROUP), F32)],
        compiler_params=_cparams(("parallel", "arbitrary")),
        name="retention",
    )(u32, u32, u32, u32, cos_t, sin_t, gn_g)


def _dense_kernel(x_ref, yf_ref, yp_ref, yr_ref, yc_ref, mod_ref, g_ref, fg_ref,
                  wo_ref, w1_ref, w3_ref, w2_ref, o_ref, *, ff_chunk, final):
    mix = _dot(yf_ref[0], wo_ref[0:GROUP, :])
    for n, y_ref in enumerate((yp_ref, yr_ref, yc_ref), start=1):
        mix = mix + _dot(y_ref[0], wo_ref[n * GROUP:(n + 1) * GROUP, :])
    x = x_ref[0] + mod_ref[0, 2:3, :] * mix
    h = _rms_mod(x, g_ref[...], mod_ref[0, 4:5, :], mod_ref[0, 3:4, :]).astype(BF16)
    d_ff = w1_ref.shape[1]
    f = None
    for c in range(0, d_ff, ff_chunk):
        a = _dot(h, w1_ref[:, c:c + ff_chunk])
        gated = (_silu(a) * _dot(h, w3_ref[:, c:c + ff_chunk])).astype(BF16)
        part = _dot(gated, w2_ref[c:c + ff_chunk, :])
        f = part if f is None else f + part
    x = x + mod_ref[0, 5:6, :] * f
    if final:
        ms = jnp.mean(x * x, axis=-1, keepdims=True)
        x = x * lax.rsqrt(ms + EPS) * fg_ref[...]
    o_ref[0] = x


def _dense(x, ys, mod_l, g, final_g, wo, w1, w3, w2, tm, final):
    b, s, d = x.shape
    d_ff = w1.shape[1]
    ff_chunk = 256
    assert d_ff % ff_chunk == 0
    const = lambda bi, i: (0, 0)
    resident = lambda shape: pl.BlockSpec(shape, const, pipeline_mode=pl.Buffered(1))
    tile = lambda w: pl.BlockSpec((1, tm, w), lambda bi, i: (bi, i, 0))
    return pl.pallas_call(
        functools.partial(_dense_kernel, ff_chunk=ff_chunk, final=final),
        out_shape=jax.ShapeDtypeStruct((b, s, d), F32),
        grid=(b, s // tm),
        in_specs=[tile(d), tile(GROUP), tile(GROUP), tile(GROUP), tile(GROUP),
                  pl.BlockSpec((1, 6, d), lambda bi, i: (bi, 0, 0)),
                  pl.BlockSpec((1, d), const), pl.BlockSpec((1, d), const),
                  resident((d, d)), resident((d, d_ff)), resident((d, d_ff)), resident((d_ff, d))],
        out_specs=tile(d),
        compiler_params=_cparams(("parallel", "parallel")),
        name="out_proj_swiglu",
    )(x, *ys, mod_l, g, final_g, wo, w1, w3, w2)


def kernel(x, c, positions, ada_w, ada_b, norm_mix_g, norm_ffn_g, w_in, fox_fb, pool_w, pool_scale,
           ret_gn_g, conv_w, conv_b, conv_ln_g, conv_ln_b, w_out, ffn_w1, ffn_w3, ffn_w2, final_g):
    b, s, d = x.shape
    depth = ada_w.shape[0]
    tm = min(s, 512)
    n_fox = 3 * GROUP
    n_ff = fox_fb.shape[1]

    mod = _ada_mod(c, ada_w, ada_b).reshape(depth, b, 6, d)
    cos_t, sin_t = _rope_tables(positions)

    for l in range(depth):
        w16 = w_in[l][:, :2 * GROUP].astype(BF16)
        wvt = w_in[l][:, 2 * GROUP:n_fox].T.astype(BF16)
        w32 = jnp.concatenate(
            [w_in[l][:, n_fox + n_ff:], w_in[l][:, n_fox:n_fox + n_ff],
             jnp.zeros((d, LANES - n_ff), F32)], axis=1).astype(BF16)
        assert w32.shape[1] == U32_WIDTH
        u16, vt, u32 = _inproj(x, mod[l], norm_mix_g[l].reshape(1, d), w16, wvt, w32, tm)

        fb_row = jnp.pad(fox_fb[l], (0, LANES - n_ff)).reshape(1, LANES)
        kx = _forget_keys(u32, u16, fb_row)
        y_fox = _fox_attention(u16, kx, vt, tm)

        pool_bd = jax.scipy.linalg.block_diag(*[pool_w[l][gi] for gi in range(len(POOL_WINDOWS))]).astype(BF16)
        y_pool = _pool_mixer(u32, pool_bd, pool_scale[l].reshape(1, GROUP), tm)

        y_ret = _retention(u32, cos_t, sin_t, ret_gn_g[l].reshape(1, GROUP), tm)

        conv_w_pad = jnp.pad(conv_w[l], ((0, CONV_HALO - CONV_WIDTH), (0, 0)))
        y_conv = _conv_mixer(u32, conv_w_pad, conv_b[l].reshape(1, GROUP), conv_ln_g[l].reshape(1, GROUP),
                             conv_ln_b[l].reshape(1, GROUP), tm)

        x = _dense(x, (y_fox, y_pool, y_ret, y_conv), mod[l], norm_ffn_g[l].reshape(1, d),
                   final_g.reshape(1, d), w_out[l].astype(BF16), ffn_w1[l].astype(BF16),
                   ffn_w3[l].astype(BF16), ffn_w2[l].astype(BF16), tm, final=(l == depth - 1))
    return x
```

```python
import functools
import math

import jax
import jax.numpy as jnp
from jax import lax
from jax.experimental import pallas as pl
from jax.experimental.pallas import tpu as pltpu

F32 = jnp.float32
BF16 = jnp.bfloat16

EPS = 1e-6
HEAD_DIM = 64
LANES = 128
SUBLANES = 8
LOG2E = math.log2(math.e)
GROUP = 256
POOL_WINDOWS = (2, 4, 8, 16)
POOL_HALO = 16
CONV_WIDTH = 31
CONV_HALO = 32
RET_CHUNK = 128
RET_HEADS = 4
ROPE_BASE = 10000.0
VMEM_LIMIT = 56 * 1024 * 1024

U32_WIDTH = 1920
COL_POOL, COL_RQ, COL_RK, COL_RV, COL_RG, COL_CA, COL_CG = 0, 1, 2, 3, 4, 5, 6
COL_FF_128 = 14


def _cparams(sem):
    return pltpu.CompilerParams(dimension_semantics=sem, vmem_limit_bytes=VMEM_LIMIT)


def _silu(x):
    return x * jax.nn.sigmoid(x)


def _rms_mod(x, g, scale, shift):
    ms = jnp.mean(x * x, axis=-1, keepdims=True)
    return (x * lax.rsqrt(ms + EPS) * g) * (1.0 + scale) + shift


def _dot(a, b):
    return jnp.dot(a, b, preferred_element_type=F32)


def _dot_nt(a, b):
    return lax.dot_general(a, b, (((1,), (1,)), ((), ())), preferred_element_type=F32)


def _dot_tn(a, b):
    return lax.dot_general(a, b, (((0,), (0,)), ((), ())), preferred_element_type=F32)


def _ada_kernel(c_ref, w_ref, b_ref, o_ref):
    ca = _silu(c_ref[...]).astype(BF16)
    o_ref[0] = _dot(ca, w_ref[0].astype(BF16)) + b_ref[0]


def _ada_mod(c, ada_w, ada_b):
    depth, d, n = ada_w.shape
    b = c.shape[0]
    tn = 1536
    return pl.pallas_call(
        _ada_kernel,
        out_shape=jax.ShapeDtypeStruct((depth, b, n), F32),
        grid=(depth, n // tn),
        in_specs=[pl.BlockSpec((b, d), lambda l, j: (0, 0)),
                  pl.BlockSpec((1, d, tn), lambda l, j: (l, 0, j)),
                  pl.BlockSpec((1, 1, tn), lambda l, j: (l, 0, j))],
        out_specs=pl.BlockSpec((1, b, tn), lambda l, j: (l, 0, j)),
        compiler_params=_cparams(("parallel", "parallel")),
        name="ada_mod",
    )(c, ada_w, ada_b.reshape(depth, 1, n))


def _rope_kernel(pos_ref, inv_ref, cos_ref, sin_ref):
    ang = pos_ref[0].astype(F32) * inv_ref[...]
    lane = lax.broadcasted_iota(jnp.int32, (1, LANES), 1)
    first_half = (lane % HEAD_DIM) < (HEAD_DIM // 2)
    cos_ref[0] = jnp.cos(ang)
    s = jnp.sin(ang)
    sin_ref[0] = jnp.where(first_half, -s, s)


def _rope_tables(positions):
    b, s = positions.shape
    half = HEAD_DIM // 2
    inv = ROPE_BASE ** (-jnp.arange(half, dtype=F32) / half)
    inv = jnp.tile(inv, LANES // half).reshape(1, LANES)
    ts = min(s, 512)
    return pl.pallas_call(
        _rope_kernel,
        out_shape=(jax.ShapeDtypeStruct((b, s, LANES), F32),) * 2,
        grid=(b, s // ts),
        in_specs=[pl.BlockSpec((1, ts, 1), lambda bi, i: (bi, i, 0)),
                  pl.BlockSpec((1, LANES), lambda bi, i: (0, 0))],
        out_specs=(pl.BlockSpec((1, ts, LANES), lambda bi, i: (bi, i, 0)),) * 2,
        compiler_params=_cparams(("parallel", "parallel")),
        name="rope_tables",
    )(positions.reshape(b, s, 1), inv)


def _inproj_kernel(x_ref, mod_ref, g_ref, w16_ref, wvt_ref, w32_ref, u16_ref, vt_ref, u32_ref):
    h = _rms_mod(x_ref[0], g_ref[...], mod_ref[0, 1:2, :], mod_ref[0, 0:1, :]).astype(BF16)
    n16 = u16_ref.shape[-1]
    for c in range(0, n16, 256):
        u16_ref[0, :, c:c + 256] = _dot(h, w16_ref[:, c:c + 256]).astype(BF16)
    vt_ref[0] = _dot_nt(wvt_ref[...], h).astype(BF16)
    n32 = u32_ref.shape[-1]
    for c in range(0, n32, 384):
        u32_ref[0, :, c:c + 384] = _dot(h, w32_ref[:, c:c + 384])


def _inproj(x, mod_l, g, w16, wvt, w32, tm):
    b, s, d = x.shape
    n16, n32 = w16.shape[1], w32.shape[1]
    const = lambda bi, i: (0, 0)
    return pl.pallas_call(
        _inproj_kernel,
        out_shape=(jax.ShapeDtypeStruct((b, s, n16), BF16), jax.ShapeDtypeStruct((b, GROUP, s), BF16),
                   jax.ShapeDtypeStruct((b, s, n32), F32)),
        grid=(b, s // tm),
        in_specs=[pl.BlockSpec((1, tm, d), lambda bi, i: (bi, i, 0)),
                  pl.BlockSpec((1, 6, d), lambda bi, i: (bi, 0, 0)),
                  pl.BlockSpec((1, d), const),
                  pl.BlockSpec((d, n16), const),
                  pl.BlockSpec((GROUP, d), const),
                  pl.BlockSpec((d, n32), const)],
        out_specs=(pl.BlockSpec((1, tm, n16), lambda bi, i: (bi, i, 0)),
                   pl.BlockSpec((1, GROUP, tm), lambda bi, i: (bi, 0, i)),
                   pl.BlockSpec((1, tm, n32), lambda bi, i: (bi, i, 0))),
        compiler_params=_cparams(("parallel", "parallel")),
        name="in_proj",
    )(x, mod_l, g, w16, wvt, w32)


BIAS_TERMS = 3
FOX_ACC_ROWS = HEAD_DIM + 16
FOX_BLOCKS_PER_STEP = 4


def _cum_kernel(ff_ref, fb_ref, k_ref, kx_ref, *, blk):
    s = ff_ref.shape[1]
    r = lax.broadcasted_iota(jnp.int32, (blk, blk), 0)
    c = lax.broadcasted_iota(jnp.int32, (blk, blk), 1)
    tri = jnp.where(c <= r, 1.0, 0.0).astype(BF16)
    pr = lax.broadcasted_iota(jnp.int32, (BIAS_TERMS * LANES, GROUP), 0)
    pc = lax.broadcasted_iota(jnp.int32, (BIAS_TERMS * LANES, GROUP), 1)
    head, term = pr % LANES, pr // LANES
    hit = (head // 2 == pc // LANES) & (pc % LANES == BIAS_TERMS * (head % 2) + term)
    place = jnp.where(hit, 1.0, 0.0).astype(BF16)

    def split(x):
        terms = []
        rest = x
        for _ in range(BIAS_TERMS):
            t = rest.astype(BF16)
            terms.append(t)
            rest = rest - t.astype(F32)
        return jnp.concatenate(terms, axis=1)

    def body(i, carry):
        r0 = pl.multiple_of(i * blk, blk)
        z = ff_ref[0, pl.ds(r0, blk), :] + fb_ref[...]
        lf = jnp.minimum(z, 0.0) - jnp.log1p(jnp.exp(-jnp.abs(z)))
        part = _dot(tri, split(lf))
        cum = carry
        for t in range(BIAS_TERMS):
            cum = cum + part[:, t * LANES:(t + 1) * LANES]
        bias = _dot(split(cum * (-LOG2E)), place)
        for pair in range(GROUP // LANES):
            base = 2 * pair * LANES
            kx_ref[0, pl.ds(r0, blk), base:base + LANES] = k_ref[0, pl.ds(r0, blk), pair * LANES:(pair + 1) * LANES]
            kx_ref[0, pl.ds(r0, blk), base + LANES:base + 2 * LANES] = (
                bias[:, pair * LANES:(pair + 1) * LANES].astype(BF16))
        return cum[blk - 1:blk, :]

    lax.fori_loop(0, s // blk, body, jnp.zeros((1, LANES), F32))


def _forget_keys(u32, u16, fb_row):
    b, s, _ = u32.shape
    blk = min(s, 256)
    return pl.pallas_call(
        functools.partial(_cum_kernel, blk=blk),
        out_shape=jax.ShapeDtypeStruct((b, s, 2 * GROUP), BF16),
        grid=(b,),
        in_specs=[pl.BlockSpec((1, s, LANES), lambda bi: (bi, 0, COL_FF_128)),
                  pl.BlockSpec((1, LANES), lambda bi: (0, 0)),
                  pl.BlockSpec((1, s, GROUP), lambda bi: (bi, 0, 1))],
        out_specs=pl.BlockSpec((1, s, 2 * GROUP), lambda bi: (bi, 0, 0)),
        compiler_params=_cparams(("parallel",)),
        name="forget_keys",
    )(u32, fb_row, u16)


def _fox_kernel(q_ref, kx_ref, vt_ref, o_ref, m_ref, acc_ref, *, tq):
    i = pl.program_id(2)
    lane = lax.broadcasted_iota(jnp.int32, (1, LANES), 1)
    lo = lane < HEAD_DIM
    q = (q_ref[0].astype(F32) * (HEAD_DIM ** -0.5 * LOG2E)).astype(BF16)
    qx = []
    for hh in range(2):
        own = jnp.where(lo == (hh == 0), 1.0, 0.0).astype(BF16)
        picks = jnp.where(lane // BIAS_TERMS == hh, 1.0, 0.0).astype(BF16)
        qx.append(jnp.concatenate([q * own, jnp.broadcast_to(picks, q.shape)], axis=1))
    m_ref[...] = jnp.full(m_ref.shape, -jnp.inf, F32)
    acc_ref[...] = jnp.zeros(acc_ref.shape, F32)
    key = lax.broadcasted_iota(jnp.int32, (tq, tq), 0)
    qry = lax.broadcasted_iota(jnp.int32, (tq, tq), 1)
    ones_rows = jnp.ones((FOX_ACC_ROWS - HEAD_DIM, tq), BF16)

    def step(blocks):
        starts = [pl.multiple_of(j * tq, tq) for j, _ in blocks]
        scores = []
        for c0, (_, diagonal) in zip(starts, blocks):
            kx = kx_ref[0, pl.ds(c0, tq), :]
            per_head = [_dot_nt(kx, qx[hh]) for hh in range(2)]
            if diagonal:
                per_head = [jnp.where(key <= qry, s, -jnp.inf) for s in per_head]
            scores.append(per_head)
        probs, alphas = [], []
        for hh in range(2):
            m_old = m_ref[hh]
            m_new = m_old
            for per_head in scores:
                m_new = jnp.maximum(m_new, jnp.max(per_head[hh], axis=0, keepdims=True))
            alphas.append(jnp.exp2(m_old - m_new))
            probs.append([jnp.exp2(per_head[hh] - m_new).astype(BF16) for per_head in scores])
            m_ref[hh] = m_new
        for hh in range(2):
            pv = None
            for c0, p in zip(starts, probs[hh]):
                vt = vt_ref[0, hh * HEAD_DIM:(hh + 1) * HEAD_DIM, pl.ds(c0, tq)]
                part = _dot(jnp.concatenate([vt, ones_rows], axis=0), p)
                pv = part if pv is None else pv + part
            acc_ref[hh] = acc_ref[hh] * alphas[hh] + pv

    nb = FOX_BLOCKS_PER_STEP

    def off_diagonal_group(t, carry):
        step([(nb * t + u, False) for u in range(nb)])
        return carry

    lax.fori_loop(0, i // nb, off_diagonal_group, 0)
    base = (i // nb) * nb
    for rem in range(nb):
        @pl.when(i % nb == rem)
        def _(rem=rem):
            step([(base + u, False) for u in range(rem)] + [(i, True)])

    out_t = jnp.concatenate(
        [acc_ref[hh, 0:HEAD_DIM, :] / acc_ref[hh, HEAD_DIM:HEAD_DIM + 1, :] for hh in range(2)], axis=0)
    o_ref[0] = out_t.T.astype(BF16)


def _fox_attention(u16, kx, vt, tq):
    b, s, _ = u16.shape
    pairs = GROUP // LANES
    return pl.pallas_call(
        functools.partial(_fox_kernel, tq=tq),
        out_shape=jax.ShapeDtypeStruct((b, s, GROUP), BF16),
        grid=(b, pairs, s // tq),
        in_specs=[pl.BlockSpec((1, tq, LANES), lambda bi, hp, i: (bi, i, hp)),
                  pl.BlockSpec((1, s, 2 * LANES), lambda bi, hp, i: (bi, 0, hp)),
                  pl.BlockSpec((1, LANES, s), lambda bi, hp, i: (bi, hp, 0))],
        out_specs=pl.BlockSpec((1, tq, LANES), lambda bi, hp, i: (bi, i, hp)),
        scratch_shapes=[pltpu.VMEM((2, 1, tq), F32), pltpu.VMEM((2, FOX_ACC_ROWS, tq), F32)],
        compiler_params=_cparams(("parallel", "parallel", "arbitrary")),
        name="fox_attention",
    )(u16, kx, vt)


def _pool_kernel(u_ref, halo_ref, w_ref, scale_ref, o_ref, *, tm):
    i = pl.program_id(1)
    x = u_ref[0]
    halo = jnp.where(i > 0, halo_ref[0], 0.0)
    e = jnp.concatenate([halo, x], axis=0)
    sums = []
    acc = e
    for k in (1, 2, 4, 8):
        acc = acc + pltpu.roll(acc, k, 0)
        sums.append(acc[POOL_HALO:, :])
    lane = lax.broadcasted_iota(jnp.int32, (1, GROUP), 1)
    grp = lane // HEAD_DIM
    wsum = jnp.where(grp == 0, sums[0], jnp.where(grp == 1, sums[1], jnp.where(grp == 2, sums[2], sums[3])))
    width = jnp.where(grp == 0, POOL_WINDOWS[0],
                      jnp.where(grp == 1, POOL_WINDOWS[1],
                                jnp.where(grp == 2, POOL_WINDOWS[2], POOL_WINDOWS[3])))
    pos = i * tm + lax.broadcasted_iota(jnp.int32, (tm, 1), 0)
    count = jnp.minimum(pos + 1, width).astype(F32)
    delta = (wsum / count - x).astype(BF16)
    o_ref[0] = (_dot(delta, w_ref[...]) * scale_ref[...]).astype(BF16)


def _pool_mixer(u32, w_bd, scale, tm):
    b, s, _ = u32.shape
    hb = tm // POOL_HALO
    return pl.pallas_call(
        functools.partial(_pool_kernel, tm=tm),
        out_shape=jax.ShapeDtypeStruct((b, s, GROUP), BF16),
        grid=(b, s // tm),
        in_specs=[pl.BlockSpec((1, tm, GROUP), lambda bi, i: (bi, i, COL_POOL)),
                  pl.BlockSpec((1, POOL_HALO, GROUP), lambda bi, i: (bi, jnp.maximum(i * hb - 1, 0), COL_POOL)),
                  pl.BlockSpec((GROUP, GROUP), lambda bi, i: (0, 0)),
                  pl.BlockSpec((1, GROUP), lambda bi, i: (0, 0))],
        out_specs=pl.BlockSpec((1, tm, GROUP), lambda bi, i: (bi, i, 0)),
        compiler_params=_cparams(("parallel", "parallel")),
        name="pool_mixer",
    )(u32, u32, w_bd, scale)


def _conv_kernel(a_ref, g_ref, ha_ref, hg_ref, w_ref, b_ref, lng_ref, lnb_ref, o_ref, h_ref, hs_ref, *, tm, rows):
    i = pl.program_id(1)
    h_ref[0:CONV_HALO, :] = jnp.where(i > 0, ha_ref[0] * jax.nn.sigmoid(hg_ref[0]), 0.0)
    h_ref[CONV_HALO:, :] = a_ref[0] * jax.nn.sigmoid(g_ref[0])
    n = hs_ref.shape[1]
    for r in range(1, SUBLANES):
        hs_ref[r - 1] = h_ref[r:r + n, :]
    first = CONV_HALO - (CONV_WIDTH - 1)
    for r0 in range(0, tm, rows):
        acc = jnp.zeros((rows, GROUP), F32) + b_ref[...]
        for j in range(CONV_WIDTH):
            shift = (first + j) % SUBLANES
            base = r0 + first + j - shift
            taps = h_ref[base:base + rows, :] if shift == 0 else hs_ref[shift - 1, base:base + rows, :]
            acc = acc + w_ref[j:j + 1, :] * taps
        mu = jnp.mean(acc, axis=-1, keepdims=True)
        d = acc - mu
        var = jnp.mean(d * d, axis=-1, keepdims=True)
        y = d * lax.rsqrt(var + EPS) * lng_ref[...] + lnb_ref[...]
        o_ref[0, r0:r0 + rows, :] = _silu(y).astype(BF16)


def _conv_mixer(u32, w, bias, ln_g, ln_b, tm):
    b, s, _ = u32.shape
    hb = tm // CONV_HALO
    rows = min(tm, 64)
    row = lambda bi, i: (0, 0)
    halo = lambda col: (lambda bi, i: (bi, jnp.maximum(i * hb - 1, 0), col))
    return pl.pallas_call(
        functools.partial(_conv_kernel, tm=tm, rows=rows),
        out_shape=jax.ShapeDtypeStruct((b, s, GROUP), BF16),
        grid=(b, s // tm),
        in_specs=[pl.BlockSpec((1, tm, GROUP), lambda bi, i: (bi, i, COL_CA)),
                  pl.BlockSpec((1, tm, GROUP), lambda bi, i: (bi, i, COL_CG)),
                  pl.BlockSpec((1, CONV_HALO, GROUP), halo(COL_CA)),
                  pl.BlockSpec((1, CONV_HALO, GROUP), halo(COL_CG)),
                  pl.BlockSpec((CONV_HALO, GROUP), row),
                  pl.BlockSpec((1, GROUP), row), pl.BlockSpec((1, GROUP), row), pl.BlockSpec((1, GROUP), row)],
        out_specs=pl.BlockSpec((1, tm, GROUP), lambda bi, i: (bi, i, 0)),
        scratch_shapes=[pltpu.VMEM((CONV_HALO + tm, GROUP), F32),
                        pltpu.VMEM((SUBLANES - 1, CONV_HALO + tm - SUBLANES, GROUP), F32)],
        compiler_params=_cparams(("parallel", "parallel")),
        name="conv_mixer",
    )(u32, u32, u32, u32, w, bias, ln_g, ln_b)


def _ret_kernel(q_ref, k_ref, v_ref, g_ref, cos_ref, sin_ref, gn_ref, o_ref, state_ref, raw_ref, *, tr):
    C = RET_CHUNK

    @pl.when(pl.program_id(1) == 0)
    def _():
        state_ref[...] = jnp.zeros(state_ref.shape, F32)

    lane = lax.broadcasted_iota(jnp.int32, (1, LANES), 1)
    lo = lane < HEAD_DIM
    first_half = (lane % HEAD_DIM) < (HEAD_DIM // 2)
    ri = lax.broadcasted_iota(jnp.int32, (C, C), 0)
    ci = lax.broadcasted_iota(jnp.int32, (C, C), 1)
    diff = (ri - ci).astype(F32)
    tok = lax.broadcasted_iota(jnp.int32, (C, 1), 0).astype(F32)
    same_head = (ri < HEAD_DIM) == (ci < HEAD_DIM)
    gi = lax.broadcasted_iota(jnp.int32, (GROUP, GROUP), 0) // HEAD_DIM
    gj = lax.broadcasted_iota(jnp.int32, (GROUP, GROUP), 1) // HEAD_DIM
    seg_mean = jnp.where(gi == gj, 1.0 / HEAD_DIM, 0.0).astype(BF16)

    def seg_mean_dot(x):
        hi = x.astype(BF16)
        lo_part = (x - hi.astype(F32)).astype(BF16)
        return _dot(hi, seg_mean) + _dot(lo_part, seg_mean)

    cos = cos_ref[0]
    sin = sin_ref[0]

    def rope(x):
        swapped = jnp.where(first_half, pltpu.roll(x, LANES - HEAD_DIM // 2, 1), pltpu.roll(x, HEAD_DIM // 2, 1))
        return x * cos + swapped * sin

    for pr in range(RET_HEADS // 2):
        lg = [math.log(1.0 - 2.0 ** (-5.0 - (2 * pr + hh))) for hh in range(2)]
        dmask = [jnp.where(diff >= 0, jnp.exp(lg[hh] * jnp.maximum(diff, 0.0)), 0.0) for hh in range(2)]
        zeta = jnp.where(lo, jnp.exp(lg[0] * (C - 1 - tok)), jnp.exp(lg[1] * (C - 1 - tok)))
        xi = jnp.where(lo, jnp.exp(lg[0] * (tok + 1)), jnp.exp(lg[1] * (tok + 1)))
        row_is_lo = lax.broadcasted_iota(jnp.int32, (LANES, 1), 0) < HEAD_DIM
        decay = jnp.where(row_is_lo, math.exp(lg[0] * C), math.exp(lg[1] * C))
        cols = slice(pr * LANES, (pr + 1) * LANES)
        qr = rope(q_ref[0, :, cols]).astype(BF16)
        kf = rope(k_ref[0, :, cols]) * (HEAD_DIM ** -0.5)
        chunks = [slice(n * C, (n + 1) * C) for n in range(tr // C)]
        kb = kf.astype(BF16)
        vb = v_ref[0, :, cols].astype(BF16)
        zero = jnp.zeros_like(qr)
        q_heads = (jnp.where(lo, qr, zero), jnp.where(lo, zero, qr))
        scores = [[(_dot_nt(q_heads[hh][rows], kb[rows]) * dmask[hh]).astype(BF16) for hh in range(2)]
                  for rows in chunks]
        kvs = [_dot_tn((kf[rows] * zeta).astype(BF16), vb[rows]) for rows in chunks]
        intra = [[_dot(sc[hh], vb[rows]) for hh in range(2)] for sc, rows in zip(scores, chunks)]
        state = state_ref[pr]
        for n, rows in enumerate(chunks):
            cross = _dot(qr[rows], state.astype(BF16)) * xi
            raw_ref[rows, cols] = jnp.where(lo, intra[n][0], intra[n][1]) + cross
            state = state * decay + jnp.where(same_head, kvs[n], 0.0)
        state_ref[pr] = state

    o = raw_ref[...]
    d = o - seg_mean_dot(o)
    var = seg_mean_dot(d * d)
    y = d * lax.rsqrt(var + EPS) * gn_ref[...]
    o_ref[0] = (_silu(g_ref[0]) * y).astype(BF16)


def _retention(u32, cos_t, sin_t, gn_g, tr):
    b, s, _ = u32.shape
    blk = lambda col: pl.BlockSpec((1, tr, GROUP), lambda bi, i: (bi, i, col))
    tab = pl.BlockSpec((1, tr, LANES), lambda bi, i: (bi, i, 0))
    return pl.pallas_call(
        functools.partial(_ret_kernel, tr=tr),
        out_shape=jax.ShapeDtypeStruct((b, s, GROUP), BF16),
        grid=(b, s // tr),
        in_specs=[blk(COL_RQ), blk(COL_RK), blk(COL_RV), blk(COL_RG), tab, tab,
                  pl.BlockSpec((1, GROUP), lambda bi, i: (0, 0))],
        out_specs=pl.BlockSpec((1, tr, GROUP), lambda bi, i: (bi, i, 0)),
        scratch_shapes=[pltpu.VMEM((RET_HEADS // 2, LANES, LANES), F32), pltpu.VMEM((tr, GROUP), F32)],
        compiler_params=_cparams(("parallel", "arbitrary")),
        name="retention",
    )(u32, u32, u32, u32, cos_t, sin_t, gn_g)


def _dense_kernel(x_ref, yf_ref, yp_ref, yr_ref, yc_ref, mod_ref, g_ref, fg_ref,
                  wo_ref, w1_ref, w3_ref, w2_ref, o_ref, *, ff_chunk, final):
    mix = _dot(yf_ref[0], wo_ref[0:GROUP, :])
    for n, y_ref in enumerate((yp_ref, yr_ref, yc_ref), start=1):
        mix = mix + _dot(y_ref[0], wo_ref[n * GROUP:(n + 1) * GROUP, :])
    x = x_ref[0] + mod_ref[0, 2:3, :] * mix
    h = _rms_mod(x, g_ref[...], mod_ref[0, 4:5, :], mod_ref[0, 3:4, :]).astype(BF16)
    d_ff = w1_ref.shape[1]
    f = None
    for c in range(0, d_ff, ff_chunk):
        a = _dot(h, w1_ref[:, c:c + ff_chunk])
        gated = (_silu(a) * _dot(h, w3_ref[:, c:c + ff_chunk])).astype(BF16)
        part = _dot(gated, w2_ref[c:c + ff_chunk, :])
        f = part if f is None else f + part
    x = x + mod_ref[0, 5:6, :] * f
    if final:
        ms = jnp.mean(x * x, axis=-1, keepdims=True)
        x = x * lax.rsqrt(ms + EPS) * fg_ref[...]
    o_ref[0] = x


def _dense(x, ys, mod_l, g, final_g, wo, w1, w3, w2, tm, final):
    b, s, d = x.shape
    d_ff = w1.shape[1]
    ff_chunk = 256
    assert d_ff % ff_chunk == 0
    const = lambda bi, i: (0, 0)
    resident = lambda shape: pl.BlockSpec(shape, const, pipeline_mode=pl.Buffered(1))
    tile = lambda w: pl.BlockSpec((1, tm, w), lambda bi, i: (bi, i, 0))
    return pl.pallas_call(
        functools.partial(_dense_kernel, ff_chunk=ff_chunk, final=final),
        out_shape=jax.ShapeDtypeStruct((b, s, d), F32),
        grid=(b, s // tm),
        in_specs=[tile(d), tile(GROUP), tile(GROUP), tile(GROUP), tile(GROUP),
                  pl.BlockSpec((1, 6, d), lambda bi, i: (bi, 0, 0)),
                  pl.BlockSpec((1, d), const), pl.BlockSpec((1, d), const),
                  resident((d, d)), resident((d, d_ff)), resident((d, d_ff)), resident((d_ff, d))],
        out_specs=tile(d),
        compiler_params=_cparams(("parallel", "parallel")),
        name="out_proj_swiglu",
    )(x, *ys, mod_l, g, final_g, wo, w1, w3, w2)


def kernel(x, c, positions, ada_w, ada_b, norm_mix_g, norm_ffn_g, w_in, fox_fb, pool_w, pool_scale,
           ret_gn_g, conv_w, conv_b, conv_ln_g, conv_ln_b, w_out, ffn_w1, ffn_w3, ffn_w2, final_g):
    b, s, d = x.shape
    depth = ada_w.shape[0]
    tm = min(s, 512)
    n_fox = 3 * GROUP
    n_ff = fox_fb.shape[1]

    mod = _ada_mod(c, ada_w, ada_b).reshape(depth, b, 6, d)
    cos_t, sin_t = _rope_tables(positions)

    for l in range(depth):
        w16 = w_in[l][:, :2 * GROUP].astype(BF16)
        wvt = w_in[l][:, 2 * GROUP:n_fox].T.astype(BF16)
        w32 = jnp.concatenate(
            [w_in[l][:, n_fox + n_ff:], w_in[l][:, n_fox:n_fox + n_ff],
             jnp.zeros((d, LANES - n_ff), F32)], axis=1).astype(BF16)
        assert w32.shape[1] == U32_WIDTH
        u16, vt, u32 = _inproj(x, mod[l], norm_mix_g[l].reshape(1, d), w16, wvt, w32, tm)

        fb_row = jnp.pad(fox_fb[l], (0, LANES - n_ff)).reshape(1, LANES)
        kx = _forget_keys(u32, u16, fb_row)
        y_fox = _fox_attention(u16, kx, vt, tm)

        pool_bd = jax.scipy.linalg.block_diag(*[pool_w[l][gi] for gi in range(len(POOL_WINDOWS))]).astype(BF16)
        y_pool = _pool_mixer(u32, pool_bd, pool_scale[l].reshape(1, GROUP), tm)

        y_ret = _retention(u32, cos_t, sin_t, ret_gn_g[l].reshape(1, GROUP), tm)

        conv_w_pad = jnp.pad(conv_w[l], ((0, CONV_HALO - CONV_WIDTH), (0, 0)))
        y_conv = _conv_mixer(u32, conv_w_pad, conv_b[l].reshape(1, GROUP), conv_ln_g[l].reshape(1, GROUP),
                             conv_ln_b[l].reshape(1, GROUP), tm)

        x = _dense(x, (y_fox, y_pool, y_ret, y_conv), mod[l], norm_ffn_g[l].reshape(1, d),
                   final_g.reshape(1, d), w_out[l].astype(BF16), ffn_w1[l].astype(BF16),
                   ffn_w3[l].astype(BF16), ffn_w2[l].astype(BF16), tm, final=(l == depth - 1))
    return x
```

```python
import functools
import math

import jax
import jax.numpy as jnp
from jax import lax
from jax.experimental import pallas as pl
from jax.experimental.pallas import tpu as pltpu

F32 = jnp.float32
BF16 = jnp.bfloat16

EPS = 1e-6
HEAD_DIM = 64
LANES = 128
SUBLANES = 8
LOG2E = math.log2(math.e)
GROUP = 256
POOL_WINDOWS = (2, 4, 8, 16)
POOL_HALO = 16
CONV_WIDTH = 31
CONV_HALO = 32
RET_CHUNK = 128
RET_HEADS = 4
ROPE_BASE = 10000.0
VMEM_LIMIT = 56 * 1024 * 1024

U32_WIDTH = 1920
COL_POOL, COL_RQ, COL_RK, COL_RV, COL_RG, COL_CA, COL_CG = 0, 1, 2, 3, 4, 5, 6
COL_FF_128 = 14


def _cparams(sem):
    return pltpu.CompilerParams(dimension_semantics=sem, vmem_limit_bytes=VMEM_LIMIT)


def _silu(x):
    return x * jax.nn.sigmoid(x)


def _rms_mod(x, g, scale, shift):
    ms = jnp.mean(x * x, axis=-1, keepdims=True)
    return (x * lax.rsqrt(ms + EPS) * g) * (1.0 + scale) + shift


def _dot(a, b):
    return jnp.dot(a, b, preferred_element_type=F32)


def _dot_nt(a, b):
    return lax.dot_general(a, b, (((1,), (1,)), ((), ())), preferred_element_type=F32)


def _dot_tn(a, b):
    return lax.dot_general(a, b, (((0,), (0,)), ((), ())), preferred_element_type=F32)


def _ada_kernel(c_ref, w_ref, b_ref, o_ref):
    ca = _silu(c_ref[...]).astype(BF16)
    o_ref[0] = _dot(ca, w_ref[0].astype(BF16)) + b_ref[0]


def _ada_mod(c, ada_w, ada_b):
    depth, d, n = ada_w.shape
    b = c.shape[0]
    tn = 1536
    return pl.pallas_call(
        _ada_kernel,
        out_shape=jax.ShapeDtypeStruct((depth, b, n), F32),
        grid=(depth, n // tn),
        in_specs=[pl.BlockSpec((b, d), lambda l, j: (0, 0)),
                  pl.BlockSpec((1, d, tn), lambda l, j: (l, 0, j)),
                  pl.BlockSpec((1, 1, tn), lambda l, j: (l, 0, j))],
        out_specs=pl.BlockSpec((1, b, tn), lambda l, j: (l, 0, j)),
        compiler_params=_cparams(("parallel", "parallel")),
        name="ada_mod",
    )(c, ada_w, ada_b.reshape(depth, 1, n))


def _rope_kernel(pos_ref, inv_ref, cos_ref, sin_ref):
    ang = pos_ref[0].astype(F32) * inv_ref[...]
    lane = lax.broadcasted_iota(jnp.int32, (1, LANES), 1)
    first_half = (lane % HEAD_DIM) < (HEAD_DIM // 2)
    cos_ref[0] = jnp.cos(ang)
    s = jnp.sin(ang)
    sin_ref[0] = jnp.where(first_half, -s, s)


def _rope_tables(positions):
    b, s = positions.shape
    half = HEAD_DIM // 2
    inv = ROPE_BASE ** (-jnp.arange(half, dtype=F32) / half)
    inv = jnp.tile(inv, LANES // half).reshape(1, LANES)
    ts = min(s, 512)
    return pl.pallas_call(
        _rope_kernel,
        out_shape=(jax.ShapeDtypeStruct((b, s, LANES), F32),) * 2,
        grid=(b, s // ts),
        in_specs=[pl.BlockSpec((1, ts, 1), lambda bi, i: (bi, i, 0)),
                  pl.BlockSpec((1, LANES), lambda bi, i: (0, 0))],
        out_specs=(pl.BlockSpec((1, ts, LANES), lambda bi, i: (bi, i, 0)),) * 2,
        compiler_params=_cparams(("parallel", "parallel")),
        name="rope_tables",
    )(positions.reshape(b, s, 1), inv)


def _inproj_kernel(x_ref, mod_ref, g_ref, w16_ref, wvt_ref, w32_ref, u16_ref, vt_ref, u32_ref):
    h = _rms_mod(x_ref[0], g_ref[...], mod_ref[0, 1:2, :], mod_ref[0, 0:1, :]).astype(BF16)
    n16 = u16_ref.shape[-1]
    for c in range(0, n16, 256):
        u16_ref[0, :, c:c + 256] = _dot(h, w16_ref[:, c:c + 256]).astype(BF16)
    vt_ref[0] = _dot_nt(wvt_ref[...], h).astype(BF16)
    n32 = u32_ref.shape[-1]
    for c in range(0, n32, 512):
        w = min(512, n32 - c)
        u32_ref[0, :, c:c + w] = _dot(h, w32_ref[:, c:c + w])


def _inproj(x, mod_l, g, w16, wvt, w32, tm):
    b, s, d = x.shape
    n16, n32 = w16.shape[1], w32.shape[1]
    const = lambda bi, i: (0, 0)
    return pl.pallas_call(
        _inproj_kernel,
        out_shape=(jax.ShapeDtypeStruct((b, s, n16), BF16), jax.ShapeDtypeStruct((b, GROUP, s), BF16),
                   jax.ShapeDtypeStruct((b, s, n32), F32)),
        grid=(b, s // tm),
        in_specs=[pl.BlockSpec((1, tm, d), lambda bi, i: (bi, i, 0)),
                  pl.BlockSpec((1, 6, d), lambda bi, i: (bi, 0, 0)),
                  pl.BlockSpec((1, d), const),
                  pl.BlockSpec((d, n16), const),
                  pl.BlockSpec((GROUP, d), const),
                  pl.BlockSpec((d, n32), const)],
        out_specs=(pl.BlockSpec((1, tm, n16), lambda bi, i: (bi, i, 0)),
                   pl.BlockSpec((1, GROUP, tm), lambda bi, i: (bi, 0, i)),
                   pl.BlockSpec((1, tm, n32), lambda bi, i: (bi, i, 0))),
        compiler_params=_cparams(("parallel", "parallel")),
        name="in_proj",
    )(x, mod_l, g, w16, wvt, w32)


BIAS_TERMS = 3
FOX_ACC_ROWS = HEAD_DIM + 16
FOX_BLOCKS_PER_STEP = 4


def _cum_kernel(ff_ref, fb_ref, k_ref, kx_ref, *, blk):
    s = ff_ref.shape[1]
    r = lax.broadcasted_iota(jnp.int32, (blk, blk), 0)
    c = lax.broadcasted_iota(jnp.int32, (blk, blk), 1)
    tri = jnp.where(c <= r, 1.0, 0.0).astype(BF16)
    pr = lax.broadcasted_iota(jnp.int32, (BIAS_TERMS * LANES, GROUP), 0)
    pc = lax.broadcasted_iota(jnp.int32, (BIAS_TERMS * LANES, GROUP), 1)
    head, term = pr % LANES, pr // LANES
    hit = (head // 2 == pc // LANES) & (pc % LANES == BIAS_TERMS * (head % 2) + term)
    place = jnp.where(hit, 1.0, 0.0).astype(BF16)

    def split(x):
        terms = []
        rest = x
        for _ in range(BIAS_TERMS):
            t = rest.astype(BF16)
            terms.append(t)
            rest = rest - t.astype(F32)
        return jnp.concatenate(terms, axis=1)

    def body(i, carry):
        r0 = pl.multiple_of(i * blk, blk)
        z = ff_ref[0, pl.ds(r0, blk), :] + fb_ref[...]
        lf = jnp.minimum(z, 0.0) - jnp.log1p(jnp.exp(-jnp.abs(z)))
        part = _dot(tri, split(lf))
        cum = carry
        for t in range(BIAS_TERMS):
            cum = cum + part[:, t * LANES:(t + 1) * LANES]
        bias = _dot(split(cum * (-LOG2E)), place)
        for pair in range(GROUP // LANES):
            base = 2 * pair * LANES
            kx_ref[0, pl.ds(r0, blk), base:base + LANES] = k_ref[0, pl.ds(r0, blk), pair * LANES:(pair + 1) * LANES]
            kx_ref[0, pl.ds(r0, blk), base + LANES:base + 2 * LANES] = (
                bias[:, pair * LANES:(pair + 1) * LANES].astype(BF16))
        return cum[blk - 1:blk, :]

    lax.fori_loop(0, s // blk, body, jnp.zeros((1, LANES), F32))


def _forget_keys(u32, u16, fb_row):
    b, s, _ = u32.shape
    blk = min(s, 256)
    return pl.pallas_call(
        functools.partial(_cum_kernel, blk=blk),
        out_shape=jax.ShapeDtypeStruct((b, s, 2 * GROUP), BF16),
        grid=(b,),
        in_specs=[pl.BlockSpec((1, s, LANES), lambda bi: (bi, 0, COL_FF_128)),
                  pl.BlockSpec((1, LANES), lambda bi: (0, 0)),
                  pl.BlockSpec((1, s, GROUP), lambda bi: (bi, 0, 1))],
        out_specs=pl.BlockSpec((1, s, 2 * GROUP), lambda bi: (bi, 0, 0)),
        compiler_params=_cparams(("parallel",)),
        name="forget_keys",
    )(u32, fb_row, u16)


def _fox_kernel(q_ref, kx_ref, vt_ref, o_ref, m_ref, acc_ref, *, tq):
    i = pl.program_id(2)
    lane = lax.broadcasted_iota(jnp.int32, (1, LANES), 1)
    lo = lane < HEAD_DIM
    q = (q_ref[0].astype(F32) * (HEAD_DIM ** -0.5 * LOG2E)).astype(BF16)
    qx = []
    for hh in range(2):
        own = jnp.where(lo == (hh == 0), 1.0, 0.0).astype(BF16)
        picks = jnp.where(lane // BIAS_TERMS == hh, 1.0, 0.0).astype(BF16)
        qx.append(jnp.concatenate([q * own, jnp.broadcast_to(picks, q.shape)], axis=1))
    m_ref[...] = jnp.full(m_ref.shape, -jnp.inf, F32)
    acc_ref[...] = jnp.zeros(acc_ref.shape, F32)
    key = lax.broadcasted_iota(jnp.int32, (tq, tq), 0)
    qry = lax.broadcasted_iota(jnp.int32, (tq, tq), 1)
    ones_rows = jnp.ones((FOX_ACC_ROWS - HEAD_DIM, tq), BF16)

    def step(blocks):
        starts = [pl.multiple_of(j * tq, tq) for j, _ in blocks]
        scores = []
        for c0, (_, diagonal) in zip(starts, blocks):
            kx = kx_ref[0, pl.ds(c0, tq), :]
            per_head = [_dot_nt(kx, qx[hh]) for hh in range(2)]
            if diagonal:
                per_head = [jnp.where(key <= qry, s, -jnp.inf) for s in per_head]
            scores.append(per_head)
        m = [m_ref[hh] for hh in range(2)]
        acc = [acc_ref[hh] for hh in range(2)]
        for c0, per_head in zip(starts, scores):
            for hh in range(2):
                s = per_head[hh]
                m_new = jnp.maximum(m[hh], jnp.max(s, axis=0, keepdims=True))
                alpha = jnp.exp2(m[hh] - m_new)
                p = jnp.exp2(s - m_new).astype(BF16)
                vt = vt_ref[0, hh * HEAD_DIM:(hh + 1) * HEAD_DIM, pl.ds(c0, tq)]
                pv = _dot(jnp.concatenate([vt, ones_rows], axis=0), p)
                acc[hh] = acc[hh] * alpha + pv
                m[hh] = m_new
        for hh in range(2):
            m_ref[hh] = m[hh]
            acc_ref[hh] = acc[hh]

    nb = FOX_BLOCKS_PER_STEP

    def off_diagonal_group(t, carry):
        step([(nb * t + u, False) for u in range(nb)])
        return carry

    lax.fori_loop(0, i // nb, off_diagonal_group, 0)
    base = (i // nb) * nb
    for rem in range(nb):
        @pl.when(i % nb == rem)
        def _(rem=rem):
            step([(base + u, False) for u in range(rem)] + [(i, True)])

    out_t = jnp.concatenate(
        [acc_ref[hh, 0:HEAD_DIM, :] / acc_ref[hh, HEAD_DIM:HEAD_DIM + 1, :] for hh in range(2)], axis=0)
    o_ref[0] = out_t.T.astype(BF16)


def _fox_attention(u16, kx, vt, tq):
    b, s, _ = u16.shape
    pairs = GROUP // LANES
    return pl.pallas_call(
        functools.partial(_fox_kernel, tq=tq),
        out_shape=jax.ShapeDtypeStruct((b, s, GROUP), BF16),
        grid=(b, pairs, s // tq),
        in_specs=[pl.BlockSpec((1, tq, LANES), lambda bi, hp, i: (bi, i, hp)),
                  pl.BlockSpec((1, s, 2 * LANES), lambda bi, hp, i: (bi, 0, hp)),
                  pl.BlockSpec((1, LANES, s), lambda bi, hp, i: (bi, hp, 0))],
        out_specs=pl.BlockSpec((1, tq, LANES), lambda bi, hp, i: (bi, i, hp)),
        scratch_shapes=[pltpu.VMEM((2, 1, tq), F32), pltpu.VMEM((2, FOX_ACC_ROWS, tq), F32)],
        compiler_params=_cparams(("parallel", "parallel", "arbitrary")),
        name="fox_attention",
    )(u16, kx, vt)


def _pool_kernel(u_ref, halo_ref, w_ref, scale_ref, o_ref, *, tm):
    i = pl.program_id(1)
    x = u_ref[0]
    halo = jnp.where(i > 0, halo_ref[0], 0.0)
    e = jnp.concatenate([halo, x], axis=0)
    sums = []
    acc = e
    for k in (1, 2, 4, 8):
        acc = acc + pltpu.roll(acc, k, 0)
        sums.append(acc[POOL_HALO:, :])
    lane = lax.broadcasted_iota(jnp.int32, (1, GROUP), 1)
    grp = lane // HEAD_DIM
    wsum = jnp.where(grp == 0, sums[0], jnp.where(grp == 1, sums[1], jnp.where(grp == 2, sums[2], sums[3])))
    width = jnp.where(grp == 0, POOL_WINDOWS[0],
                      jnp.where(grp == 1, POOL_WINDOWS[1],
                                jnp.where(grp == 2, POOL_WINDOWS[2], POOL_WINDOWS[3])))
    pos = i * tm + lax.broadcasted_iota(jnp.int32, (tm, 1), 0)
    count = jnp.minimum(pos + 1, width).astype(F32)
    delta = (wsum / count - x).astype(BF16)
    o_ref[0] = (_dot(delta, w_ref[...]) * scale_ref[...]).astype(BF16)


def _pool_mixer(u32, w_bd, scale, tm):
    b, s, _ = u32.shape
    hb = tm // POOL_HALO
    return pl.pallas_call(
        functools.partial(_pool_kernel, tm=tm),
        out_shape=jax.ShapeDtypeStruct((b, s, GROUP), BF16),
        grid=(b, s // tm),
        in_specs=[pl.BlockSpec((1, tm, GROUP), lambda bi, i: (bi, i, COL_POOL)),
                  pl.BlockSpec((1, POOL_HALO, GROUP), lambda bi, i: (bi, jnp.maximum(i * hb - 1, 0), COL_POOL)),
                  pl.BlockSpec((GROUP, GROUP), lambda bi, i: (0, 0)),
                  pl.BlockSpec((1, GROUP), lambda bi, i: (0, 0))],
        out_specs=pl.BlockSpec((1, tm, GROUP), lambda bi, i: (bi, i, 0)),
        compiler_params=_cparams(("parallel", "parallel")),
        name="pool_mixer",
    )(u32, u32, w_bd, scale)


def _conv_kernel(a_ref, g_ref, ha_ref, hg_ref, w_ref, b_ref, lng_ref, lnb_ref, o_ref, h_ref, hs_ref, *, tm, rows):
    i = pl.program_id(1)
    h_ref[0:CONV_HALO, :] = jnp.where(i > 0, ha_ref[0] * jax.nn.sigmoid(hg_ref[0]), 0.0)
    h_ref[CONV_HALO:, :] = a_ref[0] * jax.nn.sigmoid(g_ref[0])
    n = hs_ref.shape[1]
    for r in range(1, SUBLANES):
        hs_ref[r - 1] = h_ref[r:r + n, :]
    first = CONV_HALO - (CONV_WIDTH - 1)
    for r0 in range(0, tm, rows):
        acc = jnp.zeros((rows, GROUP), F32) + b_ref[...]
        for j in range(CONV_WIDTH):
            shift = (first + j) % SUBLANES
            base = r0 + first + j - shift
            taps = h_ref[base:base + rows, :] if shift == 0 else hs_ref[shift - 1, base:base + rows, :]
            acc = acc + w_ref[j:j + 1, :] * taps
        mu = jnp.mean(acc, axis=-1, keepdims=True)
        d = acc - mu
        var = jnp.mean(d * d, axis=-1, keepdims=True)
        y = d * lax.rsqrt(var + EPS) * lng_ref[...] + lnb_ref[...]
        o_ref[0, r0:r0 + rows, :] = _silu(y).astype(BF16)


def _conv_mixer(u32, w, bias, ln_g, ln_b, tm):
    b, s, _ = u32.shape
    hb = tm // CONV_HALO
    rows = min(tm, 64)
    row = lambda bi, i: (0, 0)
    halo = lambda col: (lambda bi, i: (bi, jnp.maximum(i * hb - 1, 0), col))
    return pl.pallas_call(
        functools.partial(_conv_kernel, tm=tm, rows=rows),
        out_shape=jax.ShapeDtypeStruct((b, s, GROUP), BF16),
        grid=(b, s // tm),
        in_specs=[pl.BlockSpec((1, tm, GROUP), lambda bi, i: (bi, i, COL_CA)),
                  pl.BlockSpec((1, tm, GROUP), lambda bi, i: (bi, i, COL_CG)),
                  pl.BlockSpec((1, CONV_HALO, GROUP), halo(COL_CA)),
                  pl.BlockSpec((1, CONV_HALO, GROUP), halo(COL_CG)),
                  pl.BlockSpec((CONV_HALO, GROUP), row),
                  pl.BlockSpec((1, GROUP), row), pl.BlockSpec((1, GROUP), row), pl.BlockSpec((1, GROUP), row)],
        out_specs=pl.BlockSpec((1, tm, GROUP), lambda bi, i: (bi, i, 0)),
        scratch_shapes=[pltpu.VMEM((CONV_HALO + tm, GROUP), F32),
                        pltpu.VMEM((SUBLANES - 1, CONV_HALO + tm - SUBLANES, GROUP), F32)],
        compiler_params=_cparams(("parallel", "parallel")),
        name="conv_mixer",
    )(u32, u32, u32, u32, w, bias, ln_g, ln_b)


def _ret_kernel(q_ref, k_ref, v_ref, g_ref, cos_ref, sin_ref, gn_ref, o_ref, state_ref, raw_ref, *, tr):
    C = RET_CHUNK

    @pl.when(pl.program_id(1) == 0)
    def _():
        state_ref[...] = jnp.zeros(state_ref.shape, F32)

    lane = lax.broadcasted_iota(jnp.int32, (1, LANES), 1)
    lo = lane < HEAD_DIM
    first_half = (lane % HEAD_DIM) < (HEAD_DIM // 2)
    ri = lax.broadcasted_iota(jnp.int32, (C, C), 0)
    ci = lax.broadcasted_iota(jnp.int32, (C, C), 1)
    diff = (ri - ci).astype(F32)
    tok = lax.broadcasted_iota(jnp.int32, (C, 1), 0).astype(F32)
    same_head = (ri < HEAD_DIM) == (ci < HEAD_DIM)
    gi = lax.broadcasted_iota(jnp.int32, (GROUP, GROUP), 0) // HEAD_DIM
    gj = lax.broadcasted_iota(jnp.int32, (GROUP, GROUP), 1) // HEAD_DIM
    seg_mean = jnp.where(gi == gj, 1.0 / HEAD_DIM, 0.0).astype(BF16)

    def seg_mean_dot(x):
        hi = x.astype(BF16)
        lo_part = (x - hi.astype(F32)).astype(BF16)
        return _dot(hi, seg_mean) + _dot(lo_part, seg_mean)

    cos = cos_ref[0]
    sin = sin_ref[0]

    def rope(x):
        swapped = jnp.where(first_half, pltpu.roll(x, LANES - HEAD_DIM // 2, 1), pltpu.roll(x, HEAD_DIM // 2, 1))
        return x * cos + swapped * sin

    for pr in range(RET_HEADS // 2):
        lg = [math.log(1.0 - 2.0 ** (-5.0 - (2 * pr + hh))) for hh in range(2)]
        dmask = [jnp.where(diff >= 0, jnp.exp(lg[hh] * jnp.maximum(diff, 0.0)), 0.0) for hh in range(2)]
        zeta = jnp.where(lo, jnp.exp(lg[0] * (C - 1 - tok)), jnp.exp(lg[1] * (C - 1 - tok)))
        xi = jnp.where(lo, jnp.exp(lg[0] * (tok + 1)), jnp.exp(lg[1] * (tok + 1)))
        row_is_lo = lax.broadcasted_iota(jnp.int32, (LANES, 1), 0) < HEAD_DIM
        decay = jnp.where(row_is_lo, math.exp(lg[0] * C), math.exp(lg[1] * C))
        cols = slice(pr * LANES, (pr + 1) * LANES)
        qr = rope(q_ref[0, :, cols]).astype(BF16)
        kf = rope(k_ref[0, :, cols]) * (HEAD_DIM ** -0.5)
        chunks = [slice(n * C, (n + 1) * C) for n in range(tr // C)]
        kb = kf.astype(BF16)
        vb = v_ref[0, :, cols].astype(BF16)
        zero = jnp.zeros_like(qr)
        q_heads = (jnp.where(lo, qr, zero), jnp.where(lo, zero, qr))
        scores = [[(_dot_nt(q_heads[hh][rows], kb[rows]) * dmask[hh]).astype(BF16) for hh in range(2)]
                  for rows in chunks]
        kvs = [_dot_tn((kf[rows] * zeta).astype(BF16), vb[rows]) for rows in chunks]
        intra = [[_dot(sc[hh], vb[rows]) for hh in range(2)] for sc, rows in zip(scores, chunks)]
        state = state_ref[pr]
        for n, rows in enumerate(chunks):
            cross = _dot(qr[rows], state.astype(BF16)) * xi
            raw_ref[rows, cols] = jnp.where(lo, intra[n][0], intra[n][1]) + cross
            state = state * decay + jnp.where(same_head, kvs[n], 0.0)
        state_ref[pr] = state

    o = raw_ref[...]
    d = o - seg_mean_dot(o)
    var = seg_mean_dot(d * d)
    y = d * lax.rsqrt(var + EPS) * gn_ref[...]
    o_ref[0] = (_silu(g_ref[0]) * y).astype(BF16)


def _retention(u32, cos_t, sin_t, gn_g, tr):
    b, s, _ = u32.shape
    blk = lambda col: pl.BlockSpec((1, tr, GROUP), lambda bi, i: (bi, i, col))
    tab = pl.BlockSpec((1, tr, LANES), lambda bi, i: (bi, i, 0))
    return pl.pallas_call(
        functools.partial(_ret_kernel, tr=tr),
        out_shape=jax.ShapeDtypeStruct((b, s, GROUP), BF16),
        grid=(b, s // tr),
        in_specs=[blk(COL_RQ), blk(COL_RK), blk(COL_RV), blk(COL_RG), tab, tab,
                  pl.BlockSpec((1, GROUP), lambda bi, i: (0, 0))],
        out_specs=pl.BlockSpec((1, tr, GROUP), lambda bi, i: (bi, i, 0)),
        scratch_shapes=[pltpu.VMEM((RET_HEADS // 2, LANES, LANES), F32), pltpu.VMEM((tr, GROUP), F32)],
        compiler_params=_cparams(("parallel", "arbitrary")),
        name="retention",
    )(u32, u32, u32, u32, cos_t, sin_t, gn_g)


def _dense_kernel(x_ref, yf_ref, yp_ref, yr_ref, yc_ref, mod_ref, g_ref, fg_ref,
                  wo_ref, w1_ref, w3_ref, w2_ref, o_ref, *, ff_chunk, final):
    mix = _dot(yf_ref[0], wo_ref[0:GROUP, :])
    for n, y_ref in enumerate((yp_ref, yr_ref, yc_ref), start=1):
        mix = mix + _dot(y_ref[0], wo_ref[n * GROUP:(n + 1) * GROUP, :])
    x = x_ref[0] + mod_ref[0, 2:3, :] * mix
    h = _rms_mod(x, g_ref[...], mod_ref[0, 4:5, :], mod_ref[0, 3:4, :]).astype(BF16)
    d_ff = w1_ref.shape[1]
    f = None
    for c in range(0, d_ff, ff_chunk):
        a = _dot(h, w1_ref[:, c:c + ff_chunk])
        gated = (_silu(a) * _dot(h, w3_ref[:, c:c + ff_chunk])).astype(BF16)
        part = _dot(gated, w2_ref[c:c + ff_chunk, :])
        f = part if f is None else f + part
    x = x + mod_ref[0, 5:6, :] * f
    if final:
        ms = jnp.mean(x * x, axis=-1, keepdims=True)
        x = x * lax.rsqrt(ms + EPS) * fg_ref[...]
    o_ref[0] = x


def _dense(x, ys, mod_l, g, final_g, wo, w1, w3, w2, tm, final):
    b, s, d = x.shape
    d_ff = w1.shape[1]
    ff_chunk = 256
    assert d_ff % ff_chunk == 0
    const = lambda bi, i: (0, 0)
    resident = lambda shape: pl.BlockSpec(shape, const, pipeline_mode=pl.Buffered(1))
    tile = lambda w: pl.BlockSpec((1, tm, w), lambda bi, i: (bi, i, 0))
    return pl.pallas_call(
        functools.partial(_dense_kernel, ff_chunk=ff_chunk, final=final),
        out_shape=jax.ShapeDtypeStruct((b, s, d), F32),
        grid=(b, s // tm),
        in_specs=[tile(d), tile(GROUP), tile(GROUP), tile(GROUP), tile(GROUP),
                  pl.BlockSpec((1, 6, d), lambda bi, i: (bi, 0, 0)),
                  pl.BlockSpec((1, d), const), pl.BlockSpec((1, d), const),
                  resident((d, d)), resident((d, d_ff)), resident((d, d_ff)), resident((d_ff, d))],
        out_specs=tile(d),
        compiler_params=_cparams(("parallel", "parallel")),
        name="out_proj_swiglu",
    )(x, *ys, mod_l, g, final_g, wo, w1, w3, w2)


def kernel(x, c, positions, ada_w, ada_b, norm_mix_g, norm_ffn_g, w_in, fox_fb, pool_w, pool_scale,
           ret_gn_g, conv_w, conv_b, conv_ln_g, conv_ln_b, w_out, ffn_w1, ffn_w3, ffn_w2, final_g):
    b, s, d = x.shape
    depth = ada_w.shape[0]
    tm = min(s, 512)
    n_fox = 3 * GROUP
    n_ff = fox_fb.shape[1]

    mod = _ada_mod(c, ada_w, ada_b).reshape(depth, b, 6, d)
    cos_t, sin_t = _rope_tables(positions)

    for l in range(depth):
        w16 = w_in[l][:, :2 * GROUP].astype(BF16)
        wvt = w_in[l][:, 2 * GROUP:n_fox].T.astype(BF16)
        w32 = jnp.concatenate(
            [w_in[l][:, n_fox + n_ff:], w_in[l][:, n_fox:n_fox + n_ff],
             jnp.zeros((d, LANES - n_ff), F32)], axis=1).astype(BF16)
        assert w32.shape[1] == U32_WIDTH
        u16, vt, u32 = _inproj(x, mod[l], norm_mix_g[l].reshape(1, d), w16, wvt, w32, tm)

        fb_row = jnp.pad(fox_fb[l], (0, LANES - n_ff)).reshape(1, LANES)
        kx = _forget_keys(u32, u16, fb_row)
        y_fox = _fox_attention(u16, kx, vt, tm)

        pool_bd = jax.scipy.linalg.block_diag(*[pool_w[l][gi] for gi in range(len(POOL_WINDOWS))]).astype(BF16)
        y_pool = _pool_mixer(u32, pool_bd, pool_scale[l].reshape(1, GROUP), tm)

        y_ret = _retention(u32, cos_t, sin_t, ret_gn_g[l].reshape(1, GROUP), tm)

        conv_w_pad = jnp.pad(conv_w[l], ((0, CONV_HALO - CONV_WIDTH), (0, 0)))
        y_conv = _conv_mixer(u32, conv_w_pad, conv_b[l].reshape(1, GROUP), conv_ln_g[l].reshape(1, GROUP),
                             conv_ln_b[l].reshape(1, GROUP), tm)

        x = _dense(x, (y_fox, y_pool, y_ret, y_conv), mod[l], norm_ffn_g[l].reshape(1, d),
                   final_g.reshape(1, d), w_out[l].astype(BF16), ffn_w1[l].astype(BF16),
                   ffn_w3[l].astype(BF16), ffn_w2[l].astype(BF16), tm, final=(l == depth - 1))
    return x
```

```python
import functools
import math

import jax
import jax.numpy as jnp
from jax import lax
from jax.experimental import pallas as pl
from jax.experimental.pallas import tpu as pltpu

F32 = jnp.float32
BF16 = jnp.bfloat16

EPS = 1e-6
HEAD_DIM = 64
LANES = 128
SUBLANES = 8
LOG2E = math.log2(math.e)
GROUP = 256
POOL_WINDOWS = (2, 4, 8, 16)
POOL_HALO = 16
CONV_WIDTH = 31
CONV_HALO = 32
RET_CHUNK = 128
RET_HEADS = 4
ROPE_BASE = 10000.0
VMEM_LIMIT = 56 * 1024 * 1024

U32_WIDTH = 1920
COL_POOL, COL_RQ, COL_RK, COL_RV, COL_RG, COL_CA, COL_CG = 0, 1, 2, 3, 4, 5, 6
COL_FF_128 = 14


def _cparams(sem):
    return pltpu.CompilerParams(dimension_semantics=sem, vmem_limit_bytes=VMEM_LIMIT)


def _silu(x):
    return x * jax.nn.sigmoid(x)


def _rms_mod(x, g, scale, shift):
    ms = jnp.mean(x * x, axis=-1, keepdims=True)
    return (x * lax.rsqrt(ms + EPS) * g) * (1.0 + scale) + shift


def _dot(a, b):
    return jnp.dot(a, b, preferred_element_type=F32)


def _dot_nt(a, b):
    return lax.dot_general(a, b, (((1,), (1,)), ((), ())), preferred_element_type=F32)


def _dot_tn(a, b):
    return lax.dot_general(a, b, (((0,), (0,)), ((), ())), preferred_element_type=F32)


def _ada_kernel(c_ref, w_ref, b_ref, o_ref):
    ca = _silu(c_ref[...]).astype(BF16)
    o_ref[0] = _dot(ca, w_ref[0].astype(BF16)) + b_ref[0]


def _ada_mod(c, ada_w, ada_b):
    depth, d, n = ada_w.shape
    b = c.shape[0]
    tn = 1536
    return pl.pallas_call(
        _ada_kernel,
        out_shape=jax.ShapeDtypeStruct((depth, b, n), F32),
        grid=(depth, n // tn),
        in_specs=[pl.BlockSpec((b, d), lambda l, j: (0, 0)),
                  pl.BlockSpec((1, d, tn), lambda l, j: (l, 0, j)),
                  pl.BlockSpec((1, 1, tn), lambda l, j: (l, 0, j))],
        out_specs=pl.BlockSpec((1, b, tn), lambda l, j: (l, 0, j)),
        compiler_params=_cparams(("parallel", "parallel")),
        name="ada_mod",
    )(c, ada_w, ada_b.reshape(depth, 1, n))


def _rope_kernel(pos_ref, inv_ref, cos_ref, sin_ref):
    ang = pos_ref[0].astype(F32) * inv_ref[...]
    lane = lax.broadcasted_iota(jnp.int32, (1, LANES), 1)
    first_half = (lane % HEAD_DIM) < (HEAD_DIM // 2)
    cos_ref[0] = jnp.cos(ang)
    s = jnp.sin(ang)
    sin_ref[0] = jnp.where(first_half, -s, s)


def _rope_tables(positions):
    b, s = positions.shape
    half = HEAD_DIM // 2
    inv = ROPE_BASE ** (-jnp.arange(half, dtype=F32) / half)
    inv = jnp.tile(inv, LANES // half).reshape(1, LANES)
    ts = min(s, 512)
    return pl.pallas_call(
        _rope_kernel,
        out_shape=(jax.ShapeDtypeStruct((b, s, LANES), F32),) * 2,
        grid=(b, s // ts),
        in_specs=[pl.BlockSpec((1, ts, 1), lambda bi, i: (bi, i, 0)),
                  pl.BlockSpec((1, LANES), lambda bi, i: (0, 0))],
        out_specs=(pl.BlockSpec((1, ts, LANES), lambda bi, i: (bi, i, 0)),) * 2,
        compiler_params=_cparams(("parallel", "parallel")),
        name="rope_tables",
    )(positions.reshape(b, s, 1), inv)


def _inproj_kernel(x_ref, mod_ref, g_ref, w16_ref, wvt_ref, w32_ref, u16_ref, vt_ref, u32_ref):
    h = _rms_mod(x_ref[0], g_ref[...], mod_ref[0, 1:2, :], mod_ref[0, 0:1, :]).astype(BF16)
    n16 = u16_ref.shape[-1]
    for c in range(0, n16, 256):
        u16_ref[0, :, c:c + 256] = _dot(h, w16_ref[:, c:c + 256]).astype(BF16)
    vt_ref[0] = _dot_nt(wvt_ref[...], h).astype(BF16)
    n32 = u32_ref.shape[-1]
    for c in range(0, n32, 512):
        w = min(512, n32 - c)
        u32_ref[0, :, c:c + w] = _dot(h, w32_ref[:, c:c + w])


def _layer_spec(arr, l, **kw):
    zeros = (0,) * (arr.ndim - 1)
    return pl.BlockSpec((None,) + arr.shape[1:], lambda bi, i: (l,) + zeros, **kw)


def _inproj(x, mod, g, w16, wvt, w32, l, tm):
    b, s, d = x.shape
    n16, n32 = w16.shape[-1], w32.shape[-1]
    return pl.pallas_call(
        _inproj_kernel,
        out_shape=(jax.ShapeDtypeStruct((b, s, n16), BF16), jax.ShapeDtypeStruct((b, GROUP, s), BF16),
                   jax.ShapeDtypeStruct((b, s, n32), F32)),
        grid=(b, s // tm),
        in_specs=[pl.BlockSpec((1, tm, d), lambda bi, i: (bi, i, 0)),
                  pl.BlockSpec((None, 1, 6, d), lambda bi, i: (l, bi, 0, 0)),
                  _layer_spec(g, l), _layer_spec(w16, l), _layer_spec(wvt, l), _layer_spec(w32, l)],
        out_specs=(pl.BlockSpec((1, tm, n16), lambda bi, i: (bi, i, 0)),
                   pl.BlockSpec((1, GROUP, tm), lambda bi, i: (bi, 0, i)),
                   pl.BlockSpec((1, tm, n32), lambda bi, i: (bi, i, 0))),
        compiler_params=_cparams(("parallel", "parallel")),
        name="in_proj",
    )(x, mod, g, w16, wvt, w32)


BIAS_TERMS = 3
FOX_ACC_ROWS = HEAD_DIM + 16
FOX_BLOCKS_PER_STEP = 4


def _cum_kernel(ff_ref, fb_ref, k_ref, kx_ref, *, blk):
    s = ff_ref.shape[1]
    r = lax.broadcasted_iota(jnp.int32, (blk, blk), 0)
    c = lax.broadcasted_iota(jnp.int32, (blk, blk), 1)
    tri = jnp.where(c <= r, 1.0, 0.0).astype(BF16)
    pr = lax.broadcasted_iota(jnp.int32, (BIAS_TERMS * LANES, GROUP), 0)
    pc = lax.broadcasted_iota(jnp.int32, (BIAS_TERMS * LANES, GROUP), 1)
    head, term = pr % LANES, pr // LANES
    hit = (head // 2 == pc // LANES) & (pc % LANES == BIAS_TERMS * (head % 2) + term)
    place = jnp.where(hit, 1.0, 0.0).astype(BF16)

    def split(x):
        terms = []
        rest = x
        for _ in range(BIAS_TERMS):
            t = rest.astype(BF16)
            terms.append(t)
            rest = rest - t.astype(F32)
        return jnp.concatenate(terms, axis=1)

    def body(i, carry):
        r0 = pl.multiple_of(i * blk, blk)
        z = ff_ref[0, pl.ds(r0, blk), :] + fb_ref[...]
        lf = jnp.minimum(z, 0.0) - jnp.log1p(jnp.exp(-jnp.abs(z)))
        part = _dot(tri, split(lf))
        cum = carry
        for t in range(BIAS_TERMS):
            cum = cum + part[:, t * LANES:(t + 1) * LANES]
        bias = _dot(split(cum * (-LOG2E)), place)
        for pair in range(GROUP // LANES):
            base = 2 * pair * LANES
            kx_ref[0, pl.ds(r0, blk), base:base + LANES] = k_ref[0, pl.ds(r0, blk), pair * LANES:(pair + 1) * LANES]
            kx_ref[0, pl.ds(r0, blk), base + LANES:base + 2 * LANES] = (
                bias[:, pair * LANES:(pair + 1) * LANES].astype(BF16))
        return cum[blk - 1:blk, :]

    lax.fori_loop(0, s // blk, body, jnp.zeros((1, LANES), F32))


def _forget_keys(u32, u16, fb_row):
    b, s, _ = u32.shape
    blk = min(s, 256)
    return pl.pallas_call(
        functools.partial(_cum_kernel, blk=blk),
        out_shape=jax.ShapeDtypeStruct((b, s, 2 * GROUP), BF16),
        grid=(b,),
        in_specs=[pl.BlockSpec((1, s, LANES), lambda bi: (bi, 0, COL_FF_128)),
                  pl.BlockSpec((1, LANES), lambda bi: (0, 0)),
                  pl.BlockSpec((1, s, GROUP), lambda bi: (bi, 0, 1))],
        out_specs=pl.BlockSpec((1, s, 2 * GROUP), lambda bi: (bi, 0, 0)),
        compiler_params=_cparams(("parallel",)),
        name="forget_keys",
    )(u32, fb_row, u16)


def _fox_kernel(q_ref, kx_ref, vt_ref, o_ref, m_ref, acc_ref, *, tq):
    i = pl.program_id(2)
    lane = lax.broadcasted_iota(jnp.int32, (1, LANES), 1)
    lo = lane < HEAD_DIM
    q = (q_ref[0].astype(F32) * (HEAD_DIM ** -0.5 * LOG2E)).astype(BF16)
    qx = []
    for hh in range(2):
        own = jnp.where(lo == (hh == 0), 1.0, 0.0).astype(BF16)
        picks = jnp.where(lane // BIAS_TERMS == hh, 1.0, 0.0).astype(BF16)
        qx.append(jnp.concatenate([q * own, jnp.broadcast_to(picks, q.shape)], axis=1))
    m_ref[...] = jnp.full(m_ref.shape, -jnp.inf, F32)
    acc_ref[...] = jnp.zeros(acc_ref.shape, F32)
    key = lax.broadcasted_iota(jnp.int32, (tq, tq), 0)
    qry = lax.broadcasted_iota(jnp.int32, (tq, tq), 1)
    ones_rows = jnp.ones((FOX_ACC_ROWS - HEAD_DIM, tq), BF16)

    def step(blocks):
        starts = [pl.multiple_of(j * tq, tq) for j, _ in blocks]
        scores = []
        for c0, (_, diagonal) in zip(starts, blocks):
            kx = kx_ref[0, pl.ds(c0, tq), :]
            per_head = [_dot_nt(kx, qx[hh]) for hh in range(2)]
            if diagonal:
                per_head = [jnp.where(key <= qry, s, -jnp.inf) for s in per_head]
            scores.append(per_head)
        m = [m_ref[hh] for hh in range(2)]
        acc = [acc_ref[hh] for hh in range(2)]
        for c0, per_head in zip(starts, scores):
            for hh in range(2):
                s = per_head[hh]
                m_new = jnp.maximum(m[hh], jnp.max(s, axis=0, keepdims=True))
                alpha = jnp.exp2(m[hh] - m_new)
                p = jnp.exp2(s - m_new).astype(BF16)
                vt = vt_ref[0, hh * HEAD_DIM:(hh + 1) * HEAD_DIM, pl.ds(c0, tq)]
                pv = _dot(jnp.concatenate([vt, ones_rows], axis=0), p)
                acc[hh] = acc[hh] * alpha + pv
                m[hh] = m_new
        for hh in range(2):
            m_ref[hh] = m[hh]
            acc_ref[hh] = acc[hh]

    nb = FOX_BLOCKS_PER_STEP

    def off_diagonal_group(t, carry):
        step([(nb * t + u, False) for u in range(nb)])
        return carry

    lax.fori_loop(0, i // nb, off_diagonal_group, 0)
    base = (i // nb) * nb
    for rem in range(nb):
        @pl.when(i % nb == rem)
        def _(rem=rem):
            step([(base + u, False) for u in range(rem)] + [(i, True)])

    out_t = jnp.concatenate(
        [acc_ref[hh, 0:HEAD_DIM, :] / acc_ref[hh, HEAD_DIM:HEAD_DIM + 1, :] for hh in range(2)], axis=0)
    o_ref[0] = out_t.T.astype(BF16)


def _fox_attention(u16, kx, vt, tq):
    b, s, _ = u16.shape
    pairs = GROUP // LANES
    return pl.pallas_call(
        functools.partial(_fox_kernel, tq=tq),
        out_shape=jax.ShapeDtypeStruct((b, s, GROUP), BF16),
        grid=(b, pairs, s // tq),
        in_specs=[pl.BlockSpec((1, tq, LANES), lambda bi, hp, i: (bi, i, hp)),
                  pl.BlockSpec((1, s, 2 * LANES), lambda bi, hp, i: (bi, 0, hp)),
                  pl.BlockSpec((1, LANES, s), lambda bi, hp, i: (bi, hp, 0))],
        out_specs=pl.BlockSpec((1, tq, LANES), lambda bi, hp, i: (bi, i, hp)),
        scratch_shapes=[pltpu.VMEM((2, 1, tq), F32), pltpu.VMEM((2, FOX_ACC_ROWS, tq), F32)],
        compiler_params=_cparams(("parallel", "parallel", "arbitrary")),
        name="fox_attention",
    )(u16, kx, vt)


def _pool_kernel(u_ref, halo_ref, w_ref, scale_ref, o_ref, *, tm):
    i = pl.program_id(1)
    x = u_ref[0]
    halo = jnp.where(i > 0, halo_ref[0], 0.0)
    e = jnp.concatenate([halo, x], axis=0)
    sums = []
    acc = e
    for k in (1, 2, 4, 8):
        acc = acc + pltpu.roll(acc, k, 0)
        sums.append(acc[POOL_HALO:, :])
    lane = lax.broadcasted_iota(jnp.int32, (1, GROUP), 1)
    grp = lane // HEAD_DIM
    wsum = jnp.where(grp == 0, sums[0], jnp.where(grp == 1, sums[1], jnp.where(grp == 2, sums[2], sums[3])))
    width = jnp.where(grp == 0, POOL_WINDOWS[0],
                      jnp.where(grp == 1, POOL_WINDOWS[1],
                                jnp.where(grp == 2, POOL_WINDOWS[2], POOL_WINDOWS[3])))
    pos = i * tm + lax.broadcasted_iota(jnp.int32, (tm, 1), 0)
    count = jnp.minimum(pos + 1, width).astype(F32)
    delta = (wsum / count - x).astype(BF16)
    o_ref[0] = (_dot(delta, w_ref[...]) * scale_ref[...]).astype(BF16)


def _pool_mixer(u32, w_bd, scale, tm):
    b, s, _ = u32.shape
    hb = tm // POOL_HALO
    return pl.pallas_call(
        functools.partial(_pool_kernel, tm=tm),
        out_shape=jax.ShapeDtypeStruct((b, s, GROUP), BF16),
        grid=(b, s // tm),
        in_specs=[pl.BlockSpec((1, tm, GROUP), lambda bi, i: (bi, i, COL_POOL)),
                  pl.BlockSpec((1, POOL_HALO, GROUP), lambda bi, i: (bi, jnp.maximum(i * hb - 1, 0), COL_POOL)),
                  pl.BlockSpec((GROUP, GROUP), lambda bi, i: (0, 0)),
                  pl.BlockSpec((1, GROUP), lambda bi, i: (0, 0))],
        out_specs=pl.BlockSpec((1, tm, GROUP), lambda bi, i: (bi, i, 0)),
        compiler_params=_cparams(("parallel", "parallel")),
        name="pool_mixer",
    )(u32, u32, w_bd, scale)


def _conv_kernel(a_ref, g_ref, ha_ref, hg_ref, w_ref, b_ref, lng_ref, lnb_ref, o_ref, h_ref, hs_ref, *, tm, rows):
    i = pl.program_id(1)
    h_ref[0:CONV_HALO, :] = jnp.where(i > 0, ha_ref[0] * jax.nn.sigmoid(hg_ref[0]), 0.0)
    h_ref[CONV_HALO:, :] = a_ref[0] * jax.nn.sigmoid(g_ref[0])
    n = hs_ref.shape[1]
    for r in range(1, SUBLANES):
        hs_ref[r - 1] = h_ref[r:r + n, :]
    first = CONV_HALO - (CONV_WIDTH - 1)
    for r0 in range(0, tm, rows):
        acc = jnp.zeros((rows, GROUP), F32) + b_ref[...]
        for j in range(CONV_WIDTH):
            shift = (first + j) % SUBLANES
            base = r0 + first + j - shift
            taps = h_ref[base:base + rows, :] if shift == 0 else hs_ref[shift - 1, base:base + rows, :]
            acc = acc + w_ref[j:j + 1, :] * taps
        mu = jnp.mean(acc, axis=-1, keepdims=True)
        d = acc - mu
        var = jnp.mean(d * d, axis=-1, keepdims=True)
        y = d * lax.rsqrt(var + EPS) * lng_ref[...] + lnb_ref[...]
        o_ref[0, r0:r0 + rows, :] = _silu(y).astype(BF16)


def _conv_mixer(u32, w, bias, ln_g, ln_b, tm):
    b, s, _ = u32.shape
    hb = tm // CONV_HALO
    rows = min(tm, 64)
    row = lambda bi, i: (0, 0)
    halo = lambda col: (lambda bi, i: (bi, jnp.maximum(i * hb - 1, 0), col))
    return pl.pallas_call(
        functools.partial(_conv_kernel, tm=tm, rows=rows),
        out_shape=jax.ShapeDtypeStruct((b, s, GROUP), BF16),
        grid=(b, s // tm),
        in_specs=[pl.BlockSpec((1, tm, GROUP), lambda bi, i: (bi, i, COL_CA)),
                  pl.BlockSpec((1, tm, GROUP), lambda bi, i: (bi, i, COL_CG)),
                  pl.BlockSpec((1, CONV_HALO, GROUP), halo(COL_CA)),
                  pl.BlockSpec((1, CONV_HALO, GROUP), halo(COL_CG)),
                  pl.BlockSpec((CONV_HALO, GROUP), row),
                  pl.BlockSpec((1, GROUP), row), pl.BlockSpec((1, GROUP), row), pl.BlockSpec((1, GROUP), row)],
        out_specs=pl.BlockSpec((1, tm, GROUP), lambda bi, i: (bi, i, 0)),
        scratch_shapes=[pltpu.VMEM((CONV_HALO + tm, GROUP), F32),
                        pltpu.VMEM((SUBLANES - 1, CONV_HALO + tm - SUBLANES, GROUP), F32)],
        compiler_params=_cparams(("parallel", "parallel")),
        name="conv_mixer",
    )(u32, u32, u32, u32, w, bias, ln_g, ln_b)


def _ret_kernel(q_ref, k_ref, v_ref, g_ref, cos_ref, sin_ref, gn_ref, o_ref, state_ref, raw_ref, *, tr):
    C = RET_CHUNK

    @pl.when(pl.program_id(1) == 0)
    def _():
        state_ref[...] = jnp.zeros(state_ref.shape, F32)

    lane = lax.broadcasted_iota(jnp.int32, (1, LANES), 1)
    lo = lane < HEAD_DIM
    first_half = (lane % HEAD_DIM) < (HEAD_DIM // 2)
    ri = lax.broadcasted_iota(jnp.int32, (C, C), 0)
    ci = lax.broadcasted_iota(jnp.int32, (C, C), 1)
    diff = (ri - ci).astype(F32)
    tok = lax.broadcasted_iota(jnp.int32, (C, 1), 0).astype(F32)
    same_head = (ri < HEAD_DIM) == (ci < HEAD_DIM)
    gi = lax.broadcasted_iota(jnp.int32, (GROUP, GROUP), 0) // HEAD_DIM
    gj = lax.broadcasted_iota(jnp.int32, (GROUP, GROUP), 1) // HEAD_DIM
    seg_mean = jnp.where(gi == gj, 1.0 / HEAD_DIM, 0.0).astype(BF16)

    def seg_mean_dot(x):
        hi = x.astype(BF16)
        lo_part = (x - hi.astype(F32)).astype(BF16)
        return _dot(hi, seg_mean) + _dot(lo_part, seg_mean)

    cos = cos_ref[0]
    sin = sin_ref[0]

    def rope(x):
        swapped = jnp.where(first_half, pltpu.roll(x, LANES - HEAD_DIM // 2, 1), pltpu.roll(x, HEAD_DIM // 2, 1))
        return x * cos + swapped * sin

    for pr in range(RET_HEADS // 2):
        lg = [math.log(1.0 - 2.0 ** (-5.0 - (2 * pr + hh))) for hh in range(2)]
        dmask = [jnp.where(diff >= 0, jnp.exp(lg[hh] * jnp.maximum(diff, 0.0)), 0.0) for hh in range(2)]
        zeta = jnp.where(lo, jnp.exp(lg[0] * (C - 1 - tok)), jnp.exp(lg[1] * (C - 1 - tok)))
        xi = jnp.where(lo, jnp.exp(lg[0] * (tok + 1)), jnp.exp(lg[1] * (tok + 1)))
        row_is_lo = lax.broadcasted_iota(jnp.int32, (LANES, 1), 0) < HEAD_DIM
        decay = jnp.where(row_is_lo, math.exp(lg[0] * C), math.exp(lg[1] * C))
        cols = slice(pr * LANES, (pr + 1) * LANES)
        qr = rope(q_ref[0, :, cols]).astype(BF16)
        kf = rope(k_ref[0, :, cols]) * (HEAD_DIM ** -0.5)
        chunks = [slice(n * C, (n + 1) * C) for n in range(tr // C)]
        kb = kf.astype(BF16)
        vb = v_ref[0, :, cols].astype(BF16)
        zero = jnp.zeros_like(qr)
        q_heads = (jnp.where(lo, qr, zero), jnp.where(lo, zero, qr))
        scores = [[(_dot_nt(q_heads[hh][rows], kb[rows]) * dmask[hh]).astype(BF16) for hh in range(2)]
                  for rows in chunks]
        kvs = [_dot_tn((kf[rows] * zeta).astype(BF16), vb[rows]) for rows in chunks]
        intra = [[_dot(sc[hh], vb[rows]) for hh in range(2)] for sc, rows in zip(scores, chunks)]
        state = state_ref[pr]
        for n, rows in enumerate(chunks):
            cross = _dot(qr[rows], state.astype(BF16)) * xi
            raw_ref[rows, cols] = jnp.where(lo, intra[n][0], intra[n][1]) + cross
            state = state * decay + jnp.where(same_head, kvs[n], 0.0)
        state_ref[pr] = state

    o = raw_ref[...]
    d = o - seg_mean_dot(o)
    var = seg_mean_dot(d * d)
    y = d * lax.rsqrt(var + EPS) * gn_ref[...]
    o_ref[0] = (_silu(g_ref[0]) * y).astype(BF16)


def _retention(u32, cos_t, sin_t, gn_g, tr):
    b, s, _ = u32.shape
    blk = lambda col: pl.BlockSpec((1, tr, GROUP), lambda bi, i: (bi, i, col))
    tab = pl.BlockSpec((1, tr, LANES), lambda bi, i: (bi, i, 0))
    return pl.pallas_call(
        functools.partial(_ret_kernel, tr=tr),
        out_shape=jax.ShapeDtypeStruct((b, s, GROUP), BF16),
        grid=(b, s // tr),
        in_specs=[blk(COL_RQ), blk(COL_RK), blk(COL_RV), blk(COL_RG), tab, tab,
                  pl.BlockSpec((1, GROUP), lambda bi, i: (0, 0))],
        out_specs=pl.BlockSpec((1, tr, GROUP), lambda bi, i: (bi, i, 0)),
        scratch_shapes=[pltpu.VMEM((RET_HEADS // 2, LANES, LANES), F32), pltpu.VMEM((tr, GROUP), F32)],
        compiler_params=_cparams(("parallel", "arbitrary")),
        name="retention",
    )(u32, u32, u32, u32, cos_t, sin_t, gn_g)


def _dense_kernel(x_ref, yf_ref, yp_ref, yr_ref, yc_ref, mod_ref, g_ref, fg_ref,
                  wo_ref, w1_ref, w3_ref, w2_ref, o_ref, *, ff_chunk, final):
    mix = _dot(yf_ref[0], wo_ref[0:GROUP, :])
    for n, y_ref in enumerate((yp_ref, yr_ref, yc_ref), start=1):
        mix = mix + _dot(y_ref[0], wo_ref[n * GROUP:(n + 1) * GROUP, :])
    x = x_ref[0] + mod_ref[0, 2:3, :] * mix
    h = _rms_mod(x, g_ref[...], mod_ref[0, 4:5, :], mod_ref[0, 3:4, :]).astype(BF16)
    d_ff = w1_ref.shape[1]
    f = None
    for c in range(0, d_ff, ff_chunk):
        a = _dot(h, w1_ref[:, c:c + ff_chunk])
        gated = (_silu(a) * _dot(h, w3_ref[:, c:c + ff_chunk])).astype(BF16)
        part = _dot(gated, w2_ref[c:c + ff_chunk, :])
        f = part if f is None else f + part
    x = x + mod_ref[0, 5:6, :] * f
    if final:
        ms = jnp.mean(x * x, axis=-1, keepdims=True)
        x = x * lax.rsqrt(ms + EPS) * fg_ref[...]
    o_ref[0] = x


def _dense(x, ys, mod, g, final_g, wo, w1, w3, w2, l, tm, final):
    b, s, d = x.shape
    d_ff = w1.shape[-1]
    ff_chunk = 256
    assert d_ff % ff_chunk == 0
    resident = lambda arr: _layer_spec(arr, l, pipeline_mode=pl.Buffered(1))
    tile = lambda w: pl.BlockSpec((1, tm, w), lambda bi, i: (bi, i, 0))
    return pl.pallas_call(
        functools.partial(_dense_kernel, ff_chunk=ff_chunk, final=final),
        out_shape=jax.ShapeDtypeStruct((b, s, d), F32),
        grid=(b, s // tm),
        in_specs=[tile(d), tile(GROUP), tile(GROUP), tile(GROUP), tile(GROUP),
                  pl.BlockSpec((None, 1, 6, d), lambda bi, i: (l, bi, 0, 0)),
                  _layer_spec(g, l), pl.BlockSpec((1, d), lambda bi, i: (0, 0)),
                  resident(wo), resident(w1), resident(w3), resident(w2)],
        out_specs=tile(d),
        compiler_params=_cparams(("parallel", "parallel")),
        name="out_proj_swiglu",
    )(x, *ys, mod, g, final_g, wo, w1, w3, w2)


def kernel(x, c, positions, ada_w, ada_b, norm_mix_g, norm_ffn_g, w_in, fox_fb, pool_w, pool_scale,
           ret_gn_g, conv_w, conv_b, conv_ln_g, conv_ln_b, w_out, ffn_w1, ffn_w3, ffn_w2, final_g):
    b, s, d = x.shape
    depth = ada_w.shape[0]
    tm = min(s, 512)
    t_mix = min(s, 1024)
    n_fox = 3 * GROUP
    n_ff = fox_fb.shape[1]

    mod = _ada_mod(c, ada_w, ada_b).reshape(depth, b, 6, d)
    cos_t, sin_t = _rope_tables(positions)

    w16 = w_in[:, :, :2 * GROUP].astype(BF16)
    wvt = jnp.swapaxes(w_in[:, :, 2 * GROUP:n_fox], 1, 2).astype(BF16)
    w32 = jnp.concatenate(
        [w_in[:, :, n_fox + n_ff:], w_in[:, :, n_fox:n_fox + n_ff],
         jnp.zeros((depth, d, LANES - n_ff), F32)], axis=2).astype(BF16)
    assert w32.shape[-1] == U32_WIDTH
    wo, w1, w3, w2 = (w.astype(BF16) for w in (w_out, ffn_w1, ffn_w3, ffn_w2))
    g_mix = norm_mix_g.reshape(depth, 1, d)
    g_ffn = norm_ffn_g.reshape(depth, 1, d)

    for l in range(depth):
        u16, vt, u32 = _inproj(x, mod, g_mix, w16, wvt, w32, l, tm)

        fb_row = jnp.pad(fox_fb[l], (0, LANES - n_ff)).reshape(1, LANES)
        kx = _forget_keys(u32, u16, fb_row)
        y_fox = _fox_attention(u16, kx, vt, tm)

        pool_bd = jax.scipy.linalg.block_diag(*[pool_w[l][gi] for gi in range(len(POOL_WINDOWS))]).astype(BF16)
        y_pool = _pool_mixer(u32, pool_bd, pool_scale[l].reshape(1, GROUP), t_mix)

        y_ret = _retention(u32, cos_t, sin_t, ret_gn_g[l].reshape(1, GROUP), t_mix)

        conv_w_pad = jnp.pad(conv_w[l], ((0, CONV_HALO - CONV_WIDTH), (0, 0)))
        y_conv = _conv_mixer(u32, conv_w_pad, conv_b[l].reshape(1, GROUP), conv_ln_g[l].reshape(1, GROUP),
                             conv_ln_b[l].reshape(1, GROUP), t_mix)

        x = _dense(x, (y_fox, y_pool, y_ret, y_conv), mod, g_ffn, final_g.reshape(1, d),
                   wo, w1, w3, w2, l, tm, final=(l == depth - 1))
    return x
```

```python
import functools
import math

import jax
import jax.numpy as jnp
from jax import lax
from jax.experimental import pallas as pl
from jax.experimental.pallas import tpu as pltpu

F32 = jnp.float32
BF16 = jnp.bfloat16

EPS = 1e-6
HEAD_DIM = 64
LANES = 128
SUBLANES = 8
LOG2E = math.log2(math.e)
GROUP = 256
POOL_WINDOWS = (2, 4, 8, 16)
POOL_HALO = 16
CONV_WIDTH = 31
CONV_HALO = 32
RET_CHUNK = 128
RET_HEADS = 4
ROPE_BASE = 10000.0
VMEM_LIMIT = 56 * 1024 * 1024

U32_WIDTH = 1920
COL_POOL, COL_RQ, COL_RK, COL_RV, COL_RG, COL_CA, COL_CG = 0, 1, 2, 3, 4, 5, 6
COL_FF_128 = 14


def _cparams(sem):
    return pltpu.CompilerParams(dimension_semantics=sem, vmem_limit_bytes=VMEM_LIMIT)


def _silu(x):
    return x * jax.nn.sigmoid(x)


def _rms_mod(x, g, scale, shift):
    ms = jnp.mean(x * x, axis=-1, keepdims=True)
    return (x * lax.rsqrt(ms + EPS) * g) * (1.0 + scale) + shift


def _dot(a, b):
    return jnp.dot(a, b, preferred_element_type=F32)


def _dot_nt(a, b):
    return lax.dot_general(a, b, (((1,), (1,)), ((), ())), preferred_element_type=F32)


def _dot_tn(a, b):
    return lax.dot_general(a, b, (((0,), (0,)), ((), ())), preferred_element_type=F32)


def _ada_kernel(c_ref, w_ref, b_ref, o_ref):
    ca = _silu(c_ref[...]).astype(BF16)
    o_ref[0] = _dot(ca, w_ref[0].astype(BF16)) + b_ref[0]


def _ada_mod(c, ada_w, ada_b):
    depth, d, n = ada_w.shape
    b = c.shape[0]
    tn = 1536
    return pl.pallas_call(
        _ada_kernel,
        out_shape=jax.ShapeDtypeStruct((depth, b, n), F32),
        grid=(depth, n // tn),
        in_specs=[pl.BlockSpec((b, d), lambda l, j: (0, 0)),
                  pl.BlockSpec((1, d, tn), lambda l, j: (l, 0, j)),
                  pl.BlockSpec((1, 1, tn), lambda l, j: (l, 0, j))],
        out_specs=pl.BlockSpec((1, b, tn), lambda l, j: (l, 0, j)),
        compiler_params=_cparams(("parallel", "parallel")),
        name="ada_mod",
    )(c, ada_w, ada_b.reshape(depth, 1, n))


def _rope_kernel(pos_ref, inv_ref, cos_ref, sin_ref):
    ang = pos_ref[0].astype(F32) * inv_ref[...]
    lane = lax.broadcasted_iota(jnp.int32, (1, LANES), 1)
    first_half = (lane % HEAD_DIM) < (HEAD_DIM // 2)
    cos_ref[0] = jnp.cos(ang)
    s = jnp.sin(ang)
    sin_ref[0] = jnp.where(first_half, -s, s)


def _rope_tables(positions):
    b, s = positions.shape
    half = HEAD_DIM // 2
    inv = ROPE_BASE ** (-jnp.arange(half, dtype=F32) / half)
    inv = jnp.tile(inv, LANES // half).reshape(1, LANES)
    ts = min(s, 512)
    return pl.pallas_call(
        _rope_kernel,
        out_shape=(jax.ShapeDtypeStruct((b, s, LANES), F32),) * 2,
        grid=(b, s // ts),
        in_specs=[pl.BlockSpec((1, ts, 1), lambda bi, i: (bi, i, 0)),
                  pl.BlockSpec((1, LANES), lambda bi, i: (0, 0))],
        out_specs=(pl.BlockSpec((1, ts, LANES), lambda bi, i: (bi, i, 0)),) * 2,
        compiler_params=_cparams(("parallel", "parallel")),
        name="rope_tables",
    )(positions.reshape(b, s, 1), inv)


def _inproj_kernel(x_ref, mod_ref, g_ref, w16_ref, wvt_ref, w32_ref, u16_ref, vt_ref, u32_ref):
    h = _rms_mod(x_ref[0], g_ref[...], mod_ref[0, 1:2, :], mod_ref[0, 0:1, :]).astype(BF16)
    n16 = u16_ref.shape[-1]
    for c in range(0, n16, 256):
        u16_ref[0, :, c:c + 256] = _dot(h, w16_ref[:, c:c + 256]).astype(BF16)
    vt_ref[0] = _dot_nt(wvt_ref[...], h).astype(BF16)
    n32 = u32_ref.shape[-1]
    for c in range(0, n32, 512):
        w = min(512, n32 - c)
        u32_ref[0, :, c:c + w] = _dot(h, w32_ref[:, c:c + w])


def _layer_spec(arr, l, **kw):
    zeros = (0,) * (arr.ndim - 1)
    return pl.BlockSpec((None,) + arr.shape[1:], lambda bi, i: (l,) + zeros, **kw)


def _inproj(x, mod, g, w16, wvt, w32, l, tm):
    b, s, d = x.shape
    n16, n32 = w16.shape[-1], w32.shape[-1]
    return pl.pallas_call(
        _inproj_kernel,
        out_shape=(jax.ShapeDtypeStruct((b, s, n16), BF16), jax.ShapeDtypeStruct((b, GROUP, s), BF16),
                   jax.ShapeDtypeStruct((b, s, n32), F32)),
        grid=(b, s // tm),
        in_specs=[pl.BlockSpec((1, tm, d), lambda bi, i: (bi, i, 0)),
                  pl.BlockSpec((None, 1, 6, d), lambda bi, i: (l, bi, 0, 0)),
                  _layer_spec(g, l), _layer_spec(w16, l), _layer_spec(wvt, l), _layer_spec(w32, l)],
        out_specs=(pl.BlockSpec((1, tm, n16), lambda bi, i: (bi, i, 0)),
                   pl.BlockSpec((1, GROUP, tm), lambda bi, i: (bi, 0, i)),
                   pl.BlockSpec((1, tm, n32), lambda bi, i: (bi, i, 0))),
        compiler_params=_cparams(("parallel", "parallel")),
        name="in_proj",
    )(x, mod, g, w16, wvt, w32)


BIAS_TERMS = 3
FOX_ACC_ROWS = HEAD_DIM + 16
FOX_BLOCKS_PER_STEP = 4


def _cum_kernel(ff_ref, fb_ref, k_ref, kx_ref, *, blk):
    s = ff_ref.shape[1]
    r = lax.broadcasted_iota(jnp.int32, (blk, blk), 0)
    c = lax.broadcasted_iota(jnp.int32, (blk, blk), 1)
    tri = jnp.where(c <= r, 1.0, 0.0).astype(BF16)
    pr = lax.broadcasted_iota(jnp.int32, (BIAS_TERMS * LANES, GROUP), 0)
    pc = lax.broadcasted_iota(jnp.int32, (BIAS_TERMS * LANES, GROUP), 1)
    head, term = pr % LANES, pr // LANES
    hit = (head // 2 == pc // LANES) & (pc % LANES == BIAS_TERMS * (head % 2) + term)
    place = jnp.where(hit, 1.0, 0.0).astype(BF16)

    def split(x):
        terms = []
        rest = x
        for _ in range(BIAS_TERMS):
            t = rest.astype(BF16)
            terms.append(t)
            rest = rest - t.astype(F32)
        return jnp.concatenate(terms, axis=1)

    def body(i, carry):
        r0 = pl.multiple_of(i * blk, blk)
        z = ff_ref[0, pl.ds(r0, blk), :] + fb_ref[...]
        lf = jnp.minimum(z, 0.0) - jnp.log1p(jnp.exp(-jnp.abs(z)))
        part = _dot(tri, split(lf))
        cum = carry
        for t in range(BIAS_TERMS):
            cum = cum + part[:, t * LANES:(t + 1) * LANES]
        bias = _dot(split(cum * (-LOG2E)), place)
        for pair in range(GROUP // LANES):
            base = 2 * pair * LANES
            kx_ref[0, pl.ds(r0, blk), base:base + LANES] = k_ref[0, pl.ds(r0, blk), pair * LANES:(pair + 1) * LANES]
            kx_ref[0, pl.ds(r0, blk), base + LANES:base + 2 * LANES] = (
                bias[:, pair * LANES:(pair + 1) * LANES].astype(BF16))
        return cum[blk - 1:blk, :]

    lax.fori_loop(0, s // blk, body, jnp.zeros((1, LANES), F32), unroll=min(4, s // blk))


def _forget_keys(u32, u16, fb_row):
    b, s, _ = u32.shape
    blk = min(s, 256)
    return pl.pallas_call(
        functools.partial(_cum_kernel, blk=blk),
        out_shape=jax.ShapeDtypeStruct((b, s, 2 * GROUP), BF16),
        grid=(b,),
        in_specs=[pl.BlockSpec((1, s, LANES), lambda bi: (bi, 0, COL_FF_128)),
                  pl.BlockSpec((1, LANES), lambda bi: (0, 0)),
                  pl.BlockSpec((1, s, GROUP), lambda bi: (bi, 0, 1))],
        out_specs=pl.BlockSpec((1, s, 2 * GROUP), lambda bi: (bi, 0, 0)),
        compiler_params=_cparams(("parallel",)),
        name="forget_keys",
    )(u32, fb_row, u16)


def _fox_kernel(q_ref, kx_ref, vt_ref, o_ref, m_ref, acc_ref, *, tq):
    i = pl.program_id(2)
    lane = lax.broadcasted_iota(jnp.int32, (1, LANES), 1)
    lo = lane < HEAD_DIM
    q = (q_ref[0].astype(F32) * (HEAD_DIM ** -0.5 * LOG2E)).astype(BF16)
    qx = []
    for hh in range(2):
        own = jnp.where(lo == (hh == 0), 1.0, 0.0).astype(BF16)
        picks = jnp.where(lane // BIAS_TERMS == hh, 1.0, 0.0).astype(BF16)
        qx.append(jnp.concatenate([q * own, jnp.broadcast_to(picks, q.shape)], axis=1))
    m_ref[...] = jnp.full(m_ref.shape, -jnp.inf, F32)
    acc_ref[...] = jnp.zeros(acc_ref.shape, F32)
    key = lax.broadcasted_iota(jnp.int32, (tq, tq), 0)
    qry = lax.broadcasted_iota(jnp.int32, (tq, tq), 1)
    ones_rows = jnp.ones((FOX_ACC_ROWS - HEAD_DIM, tq), BF16)

    def step(blocks):
        starts = [pl.multiple_of(j * tq, tq) for j, _ in blocks]
        scores = []
        for c0, (_, diagonal) in zip(starts, blocks):
            kx = kx_ref[0, pl.ds(c0, tq), :]
            per_head = [_dot_nt(kx, qx[hh]) for hh in range(2)]
            if diagonal:
                per_head = [jnp.where(key <= qry, s, -jnp.inf) for s in per_head]
            scores.append(per_head)
        m = [m_ref[hh] for hh in range(2)]
        acc = [acc_ref[hh] for hh in range(2)]
        for c0, per_head in zip(starts, scores):
            for hh in range(2):
                s = per_head[hh]
                m_new = jnp.maximum(m[hh], jnp.max(s, axis=0, keepdims=True))
                alpha = jnp.exp2(m[hh] - m_new)
                p = jnp.exp2(s - m_new).astype(BF16)
                vt = vt_ref[0, hh * HEAD_DIM:(hh + 1) * HEAD_DIM, pl.ds(c0, tq)]
                pv = _dot(jnp.concatenate([vt, ones_rows], axis=0), p)
                acc[hh] = acc[hh] * alpha + pv
                m[hh] = m_new
        for hh in range(2):
            m_ref[hh] = m[hh]
            acc_ref[hh] = acc[hh]

    nb = FOX_BLOCKS_PER_STEP

    def off_diagonal_group(t, carry):
        step([(nb * t + u, False) for u in range(nb)])
        return carry

    lax.fori_loop(0, i // nb, off_diagonal_group, 0)
    base = (i // nb) * nb
    for rem in range(nb):
        @pl.when(i % nb == rem)
        def _(rem=rem):
            step([(base + u, False) for u in range(rem)] + [(i, True)])

    out_t = jnp.concatenate(
        [acc_ref[hh, 0:HEAD_DIM, :] / acc_ref[hh, HEAD_DIM:HEAD_DIM + 1, :] for hh in range(2)], axis=0)
    o_ref[0] = out_t.T.astype(BF16)


def _fox_attention(u16, kx, vt, tq):
    b, s, _ = u16.shape
    pairs = GROUP // LANES
    return pl.pallas_call(
        functools.partial(_fox_kernel, tq=tq),
        out_shape=jax.ShapeDtypeStruct((b, s, GROUP), BF16),
        grid=(b, pairs, s // tq),
        in_specs=[pl.BlockSpec((1, tq, LANES), lambda bi, hp, i: (bi, i, hp)),
                  pl.BlockSpec((1, s, 2 * LANES), lambda bi, hp, i: (bi, 0, hp)),
                  pl.BlockSpec((1, LANES, s), lambda bi, hp, i: (bi, hp, 0))],
        out_specs=pl.BlockSpec((1, tq, LANES), lambda bi, hp, i: (bi, i, hp)),
        scratch_shapes=[pltpu.VMEM((2, 1, tq), F32), pltpu.VMEM((2, FOX_ACC_ROWS, tq), F32)],
        compiler_params=_cparams(("parallel", "parallel", "arbitrary")),
        name="fox_attention",
    )(u16, kx, vt)


def _pool_kernel(u_ref, halo_ref, w_ref, scale_ref, o_ref, *, tm):
    i = pl.program_id(1)
    x = u_ref[0]
    halo = jnp.where(i > 0, halo_ref[0], 0.0)
    e = jnp.concatenate([halo, x], axis=0)
    sums = []
    acc = e
    for k in (1, 2, 4, 8):
        acc = acc + pltpu.roll(acc, k, 0)
        sums.append(acc[POOL_HALO:, :])
    lane = lax.broadcasted_iota(jnp.int32, (1, GROUP), 1)
    grp = lane // HEAD_DIM
    wsum = jnp.where(grp == 0, sums[0], jnp.where(grp == 1, sums[1], jnp.where(grp == 2, sums[2], sums[3])))
    width = jnp.where(grp == 0, POOL_WINDOWS[0],
                      jnp.where(grp == 1, POOL_WINDOWS[1],
                                jnp.where(grp == 2, POOL_WINDOWS[2], POOL_WINDOWS[3])))
    pos = i * tm + lax.broadcasted_iota(jnp.int32, (tm, 1), 0)
    count = jnp.minimum(pos + 1, width).astype(F32)
    delta = (wsum / count - x).astype(BF16)
    o_ref[0] = (_dot(delta, w_ref[...]) * scale_ref[...]).astype(BF16)


def _pool_mixer(u32, w_bd, scale, tm):
    b, s, _ = u32.shape
    hb = tm // POOL_HALO
    return pl.pallas_call(
        functools.partial(_pool_kernel, tm=tm),
        out_shape=jax.ShapeDtypeStruct((b, s, GROUP), BF16),
        grid=(b, s // tm),
        in_specs=[pl.BlockSpec((1, tm, GROUP), lambda bi, i: (bi, i, COL_POOL)),
                  pl.BlockSpec((1, POOL_HALO, GROUP), lambda bi, i: (bi, jnp.maximum(i * hb - 1, 0), COL_POOL)),
                  pl.BlockSpec((GROUP, GROUP), lambda bi, i: (0, 0)),
                  pl.BlockSpec((1, GROUP), lambda bi, i: (0, 0))],
        out_specs=pl.BlockSpec((1, tm, GROUP), lambda bi, i: (bi, i, 0)),
        compiler_params=_cparams(("parallel", "parallel")),
        name="pool_mixer",
    )(u32, u32, w_bd, scale)


def _conv_kernel(a_ref, g_ref, ha_ref, hg_ref, w_ref, b_ref, lng_ref, lnb_ref, o_ref, h_ref, hs_ref, *, tm, rows):
    i = pl.program_id(1)
    h_ref[0:CONV_HALO, :] = jnp.where(i > 0, ha_ref[0] * jax.nn.sigmoid(hg_ref[0]), 0.0)
    h_ref[CONV_HALO:, :] = a_ref[0] * jax.nn.sigmoid(g_ref[0])
    n = hs_ref.shape[1]
    for r in range(1, SUBLANES):
        hs_ref[r - 1] = h_ref[r:r + n, :]
    first = CONV_HALO - (CONV_WIDTH - 1)
    for r0 in range(0, tm, rows):
        acc = jnp.zeros((rows, GROUP), F32) + b_ref[...]
        for j in range(CONV_WIDTH):
            shift = (first + j) % SUBLANES
            base = r0 + first + j - shift
            taps = h_ref[base:base + rows, :] if shift == 0 else hs_ref[shift - 1, base:base + rows, :]
            acc = acc + w_ref[j:j + 1, :] * taps
        mu = jnp.mean(acc, axis=-1, keepdims=True)
        d = acc - mu
        var = jnp.mean(d * d, axis=-1, keepdims=True)
        y = d * lax.rsqrt(var + EPS) * lng_ref[...] + lnb_ref[...]
        o_ref[0, r0:r0 + rows, :] = _silu(y).astype(BF16)


def _conv_mixer(u32, w, bias, ln_g, ln_b, tm):
    b, s, _ = u32.shape
    hb = tm // CONV_HALO
    rows = min(tm, 64)
    row = lambda bi, i: (0, 0)
    halo = lambda col: (lambda bi, i: (bi, jnp.maximum(i * hb - 1, 0), col))
    return pl.pallas_call(
        functools.partial(_conv_kernel, tm=tm, rows=rows),
        out_shape=jax.ShapeDtypeStruct((b, s, GROUP), BF16),
        grid=(b, s // tm),
        in_specs=[pl.BlockSpec((1, tm, GROUP), lambda bi, i: (bi, i, COL_CA)),
                  pl.BlockSpec((1, tm, GROUP), lambda bi, i: (bi, i, COL_CG)),
                  pl.BlockSpec((1, CONV_HALO, GROUP), halo(COL_CA)),
                  pl.BlockSpec((1, CONV_HALO, GROUP), halo(COL_CG)),
                  pl.BlockSpec((CONV_HALO, GROUP), row),
                  pl.BlockSpec((1, GROUP), row), pl.BlockSpec((1, GROUP), row), pl.BlockSpec((1, GROUP), row)],
        out_specs=pl.BlockSpec((1, tm, GROUP), lambda bi, i: (bi, i, 0)),
        scratch_shapes=[pltpu.VMEM((CONV_HALO + tm, GROUP), F32),
                        pltpu.VMEM((SUBLANES - 1, CONV_HALO + tm - SUBLANES, GROUP), F32)],
        compiler_params=_cparams(("parallel", "parallel")),
        name="conv_mixer",
    )(u32, u32, u32, u32, w, bias, ln_g, ln_b)


def _ret_kernel(q_ref, k_ref, v_ref, g_ref, cos_ref, sin_ref, gn_ref, o_ref, state_ref, raw_ref, *, tr):
    C = RET_CHUNK

    @pl.when(pl.program_id(1) == 0)
    def _():
        state_ref[...] = jnp.zeros(state_ref.shape, F32)

    lane = lax.broadcasted_iota(jnp.int32, (1, LANES), 1)
    lo = lane < HEAD_DIM
    first_half = (lane % HEAD_DIM) < (HEAD_DIM // 2)
    ri = lax.broadcasted_iota(jnp.int32, (C, C), 0)
    ci = lax.broadcasted_iota(jnp.int32, (C, C), 1)
    diff = (ri - ci).astype(F32)
    tok = lax.broadcasted_iota(jnp.int32, (C, 1), 0).astype(F32)
    same_head = (ri < HEAD_DIM) == (ci < HEAD_DIM)
    gi = lax.broadcasted_iota(jnp.int32, (GROUP, GROUP), 0) // HEAD_DIM
    gj = lax.broadcasted_iota(jnp.int32, (GROUP, GROUP), 1) // HEAD_DIM
    seg_mean = jnp.where(gi == gj, 1.0 / HEAD_DIM, 0.0).astype(BF16)

    def seg_mean_dot(x):
        hi = x.astype(BF16)
        lo_part = (x - hi.astype(F32)).astype(BF16)
        return _dot(hi, seg_mean) + _dot(lo_part, seg_mean)

    cos = cos_ref[0]
    sin = sin_ref[0]

    def rope(x):
        swapped = jnp.where(first_half, pltpu.roll(x, LANES - HEAD_DIM // 2, 1), pltpu.roll(x, HEAD_DIM // 2, 1))
        return x * cos + swapped * sin

    for pr in range(RET_HEADS // 2):
        lg = [math.log(1.0 - 2.0 ** (-5.0 - (2 * pr + hh))) for hh in range(2)]
        dmask = [jnp.where(diff >= 0, jnp.exp(lg[hh] * jnp.maximum(diff, 0.0)), 0.0) for hh in range(2)]
        zeta = jnp.where(lo, jnp.exp(lg[0] * (C - 1 - tok)), jnp.exp(lg[1] * (C - 1 - tok)))
        xi = jnp.where(lo, jnp.exp(lg[0] * (tok + 1)), jnp.exp(lg[1] * (tok + 1)))
        row_is_lo = lax.broadcasted_iota(jnp.int32, (LANES, 1), 0) < HEAD_DIM
        decay = jnp.where(row_is_lo, math.exp(lg[0] * C), math.exp(lg[1] * C))
        cols = slice(pr * LANES, (pr + 1) * LANES)
        qr = rope(q_ref[0, :, cols]).astype(BF16)
        kf = rope(k_ref[0, :, cols]) * (HEAD_DIM ** -0.5)
        chunks = [slice(n * C, (n + 1) * C) for n in range(tr // C)]
        kb = kf.astype(BF16)
        vb = v_ref[0, :, cols].astype(BF16)
        zero = jnp.zeros_like(qr)
        q_heads = (jnp.where(lo, qr, zero), jnp.where(lo, zero, qr))
        scores = [[(_dot_nt(q_heads[hh][rows], kb[rows]) * dmask[hh]).astype(BF16) for hh in range(2)]
                  for rows in chunks]
        kvs = [_dot_tn((kf[rows] * zeta).astype(BF16), vb[rows]) for rows in chunks]
        intra = [[_dot(sc[hh], vb[rows]) for hh in range(2)] for sc, rows in zip(scores, chunks)]
        state = state_ref[pr]
        for n, rows in enumerate(chunks):
            cross = _dot(qr[rows], state.astype(BF16)) * xi
            raw_ref[rows, cols] = jnp.where(lo, intra[n][0], intra[n][1]) + cross
            state = state * decay + jnp.where(same_head, kvs[n], 0.0)
        state_ref[pr] = state

    o = raw_ref[...]
    d = o - seg_mean_dot(o)
    var = seg_mean_dot(d * d)
    y = d * lax.rsqrt(var + EPS) * gn_ref[...]
    o_ref[0] = (_silu(g_ref[0]) * y).astype(BF16)


def _retention(u32, cos_t, sin_t, gn_g, tr):
    b, s, _ = u32.shape
    blk = lambda col: pl.BlockSpec((1, tr, GROUP), lambda bi, i: (bi, i, col))
    tab = pl.BlockSpec((1, tr, LANES), lambda bi, i: (bi, i, 0))
    return pl.pallas_call(
        functools.partial(_ret_kernel, tr=tr),
        out_shape=jax.ShapeDtypeStruct((b, s, GROUP), BF16),
        grid=(b, s // tr),
        in_specs=[blk(COL_RQ), blk(COL_RK), blk(COL_RV), blk(COL_RG), tab, tab,
                  pl.BlockSpec((1, GROUP), lambda bi, i: (0, 0))],
        out_specs=pl.BlockSpec((1, tr, GROUP), lambda bi, i: (bi, i, 0)),
        scratch_shapes=[pltpu.VMEM((RET_HEADS // 2, LANES, LANES), F32), pltpu.VMEM((tr, GROUP), F32)],
        compiler_params=_cparams(("parallel", "arbitrary")),
        name="retention",
    )(u32, u32, u32, u32, cos_t, sin_t, gn_g)


def _dense_kernel(x_ref, yf_ref, yp_ref, yr_ref, yc_ref, mod_ref, g_ref, fg_ref,
                  wo_ref, w1_ref, w3_ref, w2_ref, o_ref, *, ff_chunk, final):
    mix = _dot(yf_ref[0], wo_ref[0:GROUP, :])
    for n, y_ref in enumerate((yp_ref, yr_ref, yc_ref), start=1):
        mix = mix + _dot(y_ref[0], wo_ref[n * GROUP:(n + 1) * GROUP, :])
    x = x_ref[0] + mod_ref[0, 2:3, :] * mix
    h = _rms_mod(x, g_ref[...], mod_ref[0, 4:5, :], mod_ref[0, 3:4, :]).astype(BF16)
    d_ff = w1_ref.shape[1]
    f = None
    for c in range(0, d_ff, ff_chunk):
        a = _dot(h, w1_ref[:, c:c + ff_chunk])
        gated = (_silu(a) * _dot(h, w3_ref[:, c:c + ff_chunk])).astype(BF16)
        part = _dot(gated, w2_ref[c:c + ff_chunk, :])
        f = part if f is None else f + part
    x = x + mod_ref[0, 5:6, :] * f
    if final:
        ms = jnp.mean(x * x, axis=-1, keepdims=True)
        x = x * lax.rsqrt(ms + EPS) * fg_ref[...]
    o_ref[0] = x


def _dense(x, ys, mod, g, final_g, wo, w1, w3, w2, l, tm, final):
    b, s, d = x.shape
    d_ff = w1.shape[-1]
    ff_chunk = 256
    assert d_ff % ff_chunk == 0
    resident = lambda arr: _layer_spec(arr, l, pipeline_mode=pl.Buffered(1))
    tile = lambda w: pl.BlockSpec((1, tm, w), lambda bi, i: (bi, i, 0))
    return pl.pallas_call(
        functools.partial(_dense_kernel, ff_chunk=ff_chunk, final=final),
        out_shape=jax.ShapeDtypeStruct((b, s, d), F32),
        grid=(b, s // tm),
        in_specs=[tile(d), tile(GROUP), tile(GROUP), tile(GROUP), tile(GROUP),
                  pl.BlockSpec((None, 1, 6, d), lambda bi, i: (l, bi, 0, 0)),
                  _layer_spec(g, l), pl.BlockSpec((1, d), lambda bi, i: (0, 0)),
                  resident(wo), resident(w1), resident(w3), resident(w2)],
        out_specs=tile(d),
        compiler_params=_cparams(("parallel", "parallel")),
        name="out_proj_swiglu",
    )(x, *ys, mod, g, final_g, wo, w1, w3, w2)


def kernel(x, c, positions, ada_w, ada_b, norm_mix_g, norm_ffn_g, w_in, fox_fb, pool_w, pool_scale,
           ret_gn_g, conv_w, conv_b, conv_ln_g, conv_ln_b, w_out, ffn_w1, ffn_w3, ffn_w2, final_g):
    b, s, d = x.shape
    depth = ada_w.shape[0]
    tm = min(s, 512)
    t_mix = min(s, 1024)
    n_fox = 3 * GROUP
    n_ff = fox_fb.shape[1]

    mod = _ada_mod(c, ada_w, ada_b).reshape(depth, b, 6, d)
    cos_t, sin_t = _rope_tables(positions)

    w16 = w_in[:, :, :2 * GROUP].astype(BF16)
    wvt = jnp.swapaxes(w_in[:, :, 2 * GROUP:n_fox], 1, 2).astype(BF16)
    w32 = jnp.concatenate(
        [w_in[:, :, n_fox + n_ff:], w_in[:, :, n_fox:n_fox + n_ff],
         jnp.zeros((depth, d, LANES - n_ff), F32)], axis=2).astype(BF16)
    assert w32.shape[-1] == U32_WIDTH
    wo, w1, w3, w2 = (w.astype(BF16) for w in (w_out, ffn_w1, ffn_w3, ffn_w2))
    g_mix = norm_mix_g.reshape(depth, 1, d)
    g_ffn = norm_ffn_g.reshape(depth, 1, d)

    for l in range(depth):
        u16, vt, u32 = _inproj(x, mod, g_mix, w16, wvt, w32, l, tm)

        fb_row = jnp.pad(fox_fb[l], (0, LANES - n_ff)).reshape(1, LANES)
        kx = _forget_keys(u32, u16, fb_row)
        y_fox = _fox_attention(u16, kx, vt, tm)

        pool_bd = jax.scipy.linalg.block_diag(*[pool_w[l][gi] for gi in range(len(POOL_WINDOWS))]).astype(BF16)
        y_pool = _pool_mixer(u32, pool_bd, pool_scale[l].reshape(1, GROUP), t_mix)

        y_ret = _retention(u32, cos_t, sin_t, ret_gn_g[l].reshape(1, GROUP), t_mix)

        conv_w_pad = jnp.pad(conv_w[l], ((0, CONV_HALO - CONV_WIDTH), (0, 0)))
        y_conv = _conv_mixer(u32, conv_w_pad, conv_b[l].reshape(1, GROUP), conv_ln_g[l].reshape(1, GROUP),
                             conv_ln_b[l].reshape(1, GROUP), t_mix)

        x = _dense(x, (y_fox, y_pool, y_ret, y_conv), mod, g_ffn, final_g.reshape(1, d),
                   wo, w1, w3, w2, l, tm, final=(l == depth - 1))
    return x
```

```python
import functools
import math

import jax
import jax.numpy as jnp
from jax import lax
from jax.experimental import pallas as pl
from jax.experimental.pallas import tpu as pltpu

F32 = jnp.float32
BF16 = jnp.bfloat16

EPS = 1e-6
HEAD_DIM = 64
LANES = 128
SUBLANES = 8
LOG2E = math.log2(math.e)
GROUP = 256
POOL_WINDOWS = (2, 4, 8, 16)
POOL_HALO = 16
CONV_WIDTH = 31
CONV_HALO = 32
RET_CHUNK = 128
RET_HEADS = 4
ROPE_BASE = 10000.0
VMEM_LIMIT = 56 * 1024 * 1024

U32_WIDTH = 1920
COL_POOL, COL_RQ, COL_RK, COL_RV, COL_RG, COL_CA, COL_CG = 0, 1, 2, 3, 4, 5, 6
COL_FF_128 = 14


def _cparams(sem):
    return pltpu.CompilerParams(dimension_semantics=sem, vmem_limit_bytes=VMEM_LIMIT)


def _silu(x):
    return x * jax.nn.sigmoid(x)


def _rms_mod(x, g, scale, shift):
    ms = jnp.mean(x * x, axis=-1, keepdims=True)
    return (x * lax.rsqrt(ms + EPS) * g) * (1.0 + scale) + shift


def _dot(a, b):
    return jnp.dot(a, b, preferred_element_type=F32)


def _dot_nt(a, b):
    return lax.dot_general(a, b, (((1,), (1,)), ((), ())), preferred_element_type=F32)


def _dot_tn(a, b):
    return lax.dot_general(a, b, (((0,), (0,)), ((), ())), preferred_element_type=F32)


def _ada_kernel(c_ref, w_ref, b_ref, o_ref):
    ca = _silu(c_ref[...]).astype(BF16)
    o_ref[0] = _dot(ca, w_ref[0].astype(BF16)) + b_ref[0]


def _ada_mod(c, ada_w, ada_b):
    depth, d, n = ada_w.shape
    b = c.shape[0]
    tn = 1536
    return pl.pallas_call(
        _ada_kernel,
        out_shape=jax.ShapeDtypeStruct((depth, b, n), F32),
        grid=(depth, n // tn),
        in_specs=[pl.BlockSpec((b, d), lambda l, j: (0, 0)),
                  pl.BlockSpec((1, d, tn), lambda l, j: (l, 0, j)),
                  pl.BlockSpec((1, 1, tn), lambda l, j: (l, 0, j))],
        out_specs=pl.BlockSpec((1, b, tn), lambda l, j: (l, 0, j)),
        compiler_params=_cparams(("parallel", "parallel")),
        name="ada_mod",
    )(c, ada_w, ada_b.reshape(depth, 1, n))


def _rope_kernel(pos_ref, inv_ref, cos_ref, sin_ref):
    ang = pos_ref[0].astype(F32) * inv_ref[...]
    lane = lax.broadcasted_iota(jnp.int32, (1, LANES), 1)
    first_half = (lane % HEAD_DIM) < (HEAD_DIM // 2)
    cos_ref[0] = jnp.cos(ang)
    s = jnp.sin(ang)
    sin_ref[0] = jnp.where(first_half, -s, s)


def _rope_tables(positions):
    b, s = positions.shape
    half = HEAD_DIM // 2
    inv = ROPE_BASE ** (-jnp.arange(half, dtype=F32) / half)
    inv = jnp.tile(inv, LANES // half).reshape(1, LANES)
    ts = min(s, 512)
    return pl.pallas_call(
        _rope_kernel,
        out_shape=(jax.ShapeDtypeStruct((b, s, LANES), F32),) * 2,
        grid=(b, s // ts),
        in_specs=[pl.BlockSpec((1, ts, 1), lambda bi, i: (bi, i, 0)),
                  pl.BlockSpec((1, LANES), lambda bi, i: (0, 0))],
        out_specs=(pl.BlockSpec((1, ts, LANES), lambda bi, i: (bi, i, 0)),) * 2,
        compiler_params=_cparams(("parallel", "parallel")),
        name="rope_tables",
    )(positions.reshape(b, s, 1), inv)


def _inproj_kernel(x_ref, mod_ref, g_ref, w16_ref, wvt_ref, w32_ref, u16_ref, vt_ref, u32_ref):
    h = _rms_mod(x_ref[0], g_ref[...], mod_ref[0, 1:2, :], mod_ref[0, 0:1, :]).astype(BF16)
    n16 = u16_ref.shape[-1]
    for c in range(0, n16, 256):
        u16_ref[0, :, c:c + 256] = _dot(h, w16_ref[:, c:c + 256]).astype(BF16)
    vt_ref[0] = _dot_nt(wvt_ref[...], h).astype(BF16)
    n32 = u32_ref.shape[-1]
    for c in range(0, n32, 512):
        w = min(512, n32 - c)
        u32_ref[0, :, c:c + w] = _dot(h, w32_ref[:, c:c + w])


def _layer_spec(arr, l, **kw):
    zeros = (0,) * (arr.ndim - 1)
    return pl.BlockSpec((None,) + arr.shape[1:], lambda bi, i: (l,) + zeros, **kw)


def _inproj(x, mod, g, w16, wvt, w32, l, tm):
    b, s, d = x.shape
    n16, n32 = w16.shape[-1], w32.shape[-1]
    return pl.pallas_call(
        _inproj_kernel,
        out_shape=(jax.ShapeDtypeStruct((b, s, n16), BF16), jax.ShapeDtypeStruct((b, GROUP, s), BF16),
                   jax.ShapeDtypeStruct((b, s, n32), F32)),
        grid=(b, s // tm),
        in_specs=[pl.BlockSpec((1, tm, d), lambda bi, i: (bi, i, 0)),
                  pl.BlockSpec((None, 1, 6, d), lambda bi, i: (l, bi, 0, 0)),
                  _layer_spec(g, l), _layer_spec(w16, l), _layer_spec(wvt, l), _layer_spec(w32, l)],
        out_specs=(pl.BlockSpec((1, tm, n16), lambda bi, i: (bi, i, 0)),
                   pl.BlockSpec((1, GROUP, tm), lambda bi, i: (bi, 0, i)),
                   pl.BlockSpec((1, tm, n32), lambda bi, i: (bi, i, 0))),
        compiler_params=_cparams(("parallel", "parallel")),
        name="in_proj",
    )(x, mod, g, w16, wvt, w32)


BIAS_TERMS = 3
FOX_ACC_ROWS = HEAD_DIM + 16
FOX_BLOCKS_PER_STEP = 4


def _cum_kernel(ff_ref, fb_ref, k_ref, kx_ref, *, blk):
    s = ff_ref.shape[1]
    r = lax.broadcasted_iota(jnp.int32, (blk, blk), 0)
    c = lax.broadcasted_iota(jnp.int32, (blk, blk), 1)
    tri = jnp.where(c <= r, 1.0, 0.0).astype(BF16)
    pr = lax.broadcasted_iota(jnp.int32, (BIAS_TERMS * LANES, GROUP), 0)
    pc = lax.broadcasted_iota(jnp.int32, (BIAS_TERMS * LANES, GROUP), 1)
    head, term = pr % LANES, pr // LANES
    hit = (head // 2 == pc // LANES) & (pc % LANES == BIAS_TERMS * (head % 2) + term)
    place = jnp.where(hit, 1.0, 0.0).astype(BF16)

    def split(x):
        terms = []
        rest = x
        for _ in range(BIAS_TERMS):
            t = rest.astype(BF16)
            terms.append(t)
            rest = rest - t.astype(F32)
        return jnp.concatenate(terms, axis=1)

    def body(i, carry):
        r0 = pl.multiple_of(i * blk, blk)
        z = ff_ref[0, pl.ds(r0, blk), :] + fb_ref[...]
        lf = jnp.minimum(z, 0.0) - jnp.log1p(jnp.exp(-jnp.abs(z)))
        part = _dot(tri, split(lf))
        cum = carry
        for t in range(BIAS_TERMS):
            cum = cum + part[:, t * LANES:(t + 1) * LANES]
        bias = _dot(split(cum * (-LOG2E)), place)
        for pair in range(GROUP // LANES):
            base = 2 * pair * LANES
            kx_ref[0, pl.ds(r0, blk), base:base + LANES] = k_ref[0, pl.ds(r0, blk), pair * LANES:(pair + 1) * LANES]
            kx_ref[0, pl.ds(r0, blk), base + LANES:base + 2 * LANES] = (
                bias[:, pair * LANES:(pair + 1) * LANES].astype(BF16))
        return cum[blk - 1:blk, :]

    lax.fori_loop(0, s // blk, body, jnp.zeros((1, LANES), F32), unroll=min(4, s // blk))


def _forget_keys(u32, u16, fb_row):
    b, s, _ = u32.shape
    blk = min(s, 256)
    return pl.pallas_call(
        functools.partial(_cum_kernel, blk=blk),
        out_shape=jax.ShapeDtypeStruct((b, s, 2 * GROUP), BF16),
        grid=(b,),
        in_specs=[pl.BlockSpec((1, s, LANES), lambda bi: (bi, 0, COL_FF_128)),
                  pl.BlockSpec((1, LANES), lambda bi: (0, 0)),
                  pl.BlockSpec((1, s, GROUP), lambda bi: (bi, 0, 1))],
        out_specs=pl.BlockSpec((1, s, 2 * GROUP), lambda bi: (bi, 0, 0)),
        compiler_params=_cparams(("parallel",)),
        name="forget_keys",
    )(u32, fb_row, u16)


def _fox_kernel(q_ref, kx_ref, vt_ref, o_ref, m_ref, acc_ref, *, tq):
    i = pl.program_id(2)
    lane = lax.broadcasted_iota(jnp.int32, (1, LANES), 1)
    lo = lane < HEAD_DIM
    q = (q_ref[0].astype(F32) * (HEAD_DIM ** -0.5 * LOG2E)).astype(BF16)
    qx = []
    for hh in range(2):
        own = jnp.where(lo == (hh == 0), 1.0, 0.0).astype(BF16)
        picks = jnp.where(lane // BIAS_TERMS == hh, 1.0, 0.0).astype(BF16)
        qx.append(jnp.concatenate([q * own, jnp.broadcast_to(picks, q.shape)], axis=1))
    m_ref[...] = jnp.full(m_ref.shape, -jnp.inf, F32)
    acc_ref[...] = jnp.zeros(acc_ref.shape, F32)
    key = lax.broadcasted_iota(jnp.int32, (tq, tq), 0)
    qry = lax.broadcasted_iota(jnp.int32, (tq, tq), 1)
    ones_rows = jnp.ones((FOX_ACC_ROWS - HEAD_DIM, tq), BF16)

    def step(blocks):
        starts = [pl.multiple_of(j * tq, tq) for j, _ in blocks]
        scores = []
        for c0, (_, diagonal) in zip(starts, blocks):
            kx = kx_ref[0, pl.ds(c0, tq), :]
            per_head = [_dot_nt(kx, qx[hh]) for hh in range(2)]
            if diagonal:
                per_head = [jnp.where(key <= qry, s, -jnp.inf) for s in per_head]
            scores.append(per_head)
        m = [m_ref[hh] for hh in range(2)]
        acc = [acc_ref[hh] for hh in range(2)]
        for c0, per_head in zip(starts, scores):
            for hh in range(2):
                s = per_head[hh]
                m_new = jnp.maximum(m[hh], jnp.max(s, axis=0, keepdims=True))
                alpha = jnp.exp2(m[hh] - m_new)
                p = jnp.exp2(s - m_new).astype(BF16)
                vt = vt_ref[0, hh * HEAD_DIM:(hh + 1) * HEAD_DIM, pl.ds(c0, tq)]
                pv = _dot(jnp.concatenate([vt, ones_rows], axis=0), p)
                acc[hh] = acc[hh] * alpha + pv
                m[hh] = m_new
        for hh in range(2):
            m_ref[hh] = m[hh]
            acc_ref[hh] = acc[hh]

    nb = FOX_BLOCKS_PER_STEP

    def off_diagonal_group(t, carry):
        step([(nb * t + u, False) for u in range(nb)])
        return carry

    lax.fori_loop(0, i // nb, off_diagonal_group, 0)
    base = (i // nb) * nb
    for rem in range(nb):
        @pl.when(i % nb == rem)
        def _(rem=rem):
            step([(base + u, False) for u in range(rem)] + [(i, True)])

    out_t = jnp.concatenate(
        [acc_ref[hh, 0:HEAD_DIM, :] / acc_ref[hh, HEAD_DIM:HEAD_DIM + 1, :] for hh in range(2)], axis=0)
    o_ref[0] = out_t.T.astype(BF16)


def _fox_attention(u16, kx, vt, tq):
    b, s, _ = u16.shape
    pairs = GROUP // LANES
    return pl.pallas_call(
        functools.partial(_fox_kernel, tq=tq),
        out_shape=jax.ShapeDtypeStruct((b, s, GROUP), BF16),
        grid=(b, pairs, s // tq),
        in_specs=[pl.BlockSpec((1, tq, LANES), lambda bi, hp, i: (bi, i, hp)),
                  pl.BlockSpec((1, s, 2 * LANES), lambda bi, hp, i: (bi, 0, hp)),
                  pl.BlockSpec((1, LANES, s), lambda bi, hp, i: (bi, hp, 0))],
        out_specs=pl.BlockSpec((1, tq, LANES), lambda bi, hp, i: (bi, i, hp)),
        scratch_shapes=[pltpu.VMEM((2, 1, tq), F32), pltpu.VMEM((2, FOX_ACC_ROWS, tq), F32)],
        compiler_params=_cparams(("parallel", "parallel", "arbitrary")),
        name="fox_attention",
    )(u16, kx, vt)


def _pool_kernel(u_ref, halo_ref, w_ref, scale_ref, o_ref, *, tm):
    i = pl.program_id(1)
    x = u_ref[0]
    halo = jnp.where(i > 0, halo_ref[0], 0.0)
    e = jnp.concatenate([halo, x], axis=0)
    sums = []
    acc = e
    for k in (1, 2, 4, 8):
        acc = acc + pltpu.roll(acc, k, 0)
        sums.append(acc[POOL_HALO:, :])
    lane = lax.broadcasted_iota(jnp.int32, (1, GROUP), 1)
    grp = lane // HEAD_DIM
    wsum = jnp.where(grp == 0, sums[0], jnp.where(grp == 1, sums[1], jnp.where(grp == 2, sums[2], sums[3])))
    width = jnp.where(grp == 0, POOL_WINDOWS[0],
                      jnp.where(grp == 1, POOL_WINDOWS[1],
                                jnp.where(grp == 2, POOL_WINDOWS[2], POOL_WINDOWS[3])))
    pos = i * tm + lax.broadcasted_iota(jnp.int32, (tm, 1), 0)
    count = jnp.minimum(pos + 1, width).astype(F32)
    delta = (wsum / count - x).astype(BF16)
    o_ref[0] = (_dot(delta, w_ref[...]) * scale_ref[...]).astype(BF16)


def _pool_mixer(u32, w_bd, scale, tm):
    b, s, _ = u32.shape
    hb = tm // POOL_HALO
    return pl.pallas_call(
        functools.partial(_pool_kernel, tm=tm),
        out_shape=jax.ShapeDtypeStruct((b, s, GROUP), BF16),
        grid=(b, s // tm),
        in_specs=[pl.BlockSpec((1, tm, GROUP), lambda bi, i: (bi, i, COL_POOL)),
                  pl.BlockSpec((1, POOL_HALO, GROUP), lambda bi, i: (bi, jnp.maximum(i * hb - 1, 0), COL_POOL)),
                  pl.BlockSpec((GROUP, GROUP), lambda bi, i: (0, 0)),
                  pl.BlockSpec((1, GROUP), lambda bi, i: (0, 0))],
        out_specs=pl.BlockSpec((1, tm, GROUP), lambda bi, i: (bi, i, 0)),
        compiler_params=_cparams(("parallel", "parallel")),
        name="pool_mixer",
    )(u32, u32, w_bd, scale)


def _conv_kernel(a_ref, g_ref, ha_ref, hg_ref, w_ref, b_ref, lng_ref, lnb_ref, o_ref, h_ref, hs_ref, *, tm, rows):
    i = pl.program_id(1)
    h_ref[0:CONV_HALO, :] = jnp.where(i > 0, ha_ref[0] * jax.nn.sigmoid(hg_ref[0]), 0.0)
    h_ref[CONV_HALO:, :] = a_ref[0] * jax.nn.sigmoid(g_ref[0])
    n = hs_ref.shape[1]
    for r in range(1, SUBLANES):
        hs_ref[r - 1] = h_ref[r:r + n, :]
    first = CONV_HALO - (CONV_WIDTH - 1)
    for r0 in range(0, tm, rows):
        acc = jnp.zeros((rows, GROUP), F32) + b_ref[...]
        for j in range(CONV_WIDTH):
            shift = (first + j) % SUBLANES
            base = r0 + first + j - shift
            taps = h_ref[base:base + rows, :] if shift == 0 else hs_ref[shift - 1, base:base + rows, :]
            acc = acc + w_ref[j:j + 1, :] * taps
        mu = jnp.mean(acc, axis=-1, keepdims=True)
        d = acc - mu
        var = jnp.mean(d * d, axis=-1, keepdims=True)
        y = d * lax.rsqrt(var + EPS) * lng_ref[...] + lnb_ref[...]
        o_ref[0, r0:r0 + rows, :] = _silu(y).astype(BF16)


def _conv_mixer(u32, w, bias, ln_g, ln_b, tm):
    b, s, _ = u32.shape
    hb = tm // CONV_HALO
    rows = min(tm, 64)
    row = lambda bi, i: (0, 0)
    halo = lambda col: (lambda bi, i: (bi, jnp.maximum(i * hb - 1, 0), col))
    return pl.pallas_call(
        functools.partial(_conv_kernel, tm=tm, rows=rows),
        out_shape=jax.ShapeDtypeStruct((b, s, GROUP), BF16),
        grid=(b, s // tm),
        in_specs=[pl.BlockSpec((1, tm, GROUP), lambda bi, i: (bi, i, COL_CA)),
                  pl.BlockSpec((1, tm, GROUP), lambda bi, i: (bi, i, COL_CG)),
                  pl.BlockSpec((1, CONV_HALO, GROUP), halo(COL_CA)),
                  pl.BlockSpec((1, CONV_HALO, GROUP), halo(COL_CG)),
                  pl.BlockSpec((CONV_HALO, GROUP), row),
                  pl.BlockSpec((1, GROUP), row), pl.BlockSpec((1, GROUP), row), pl.BlockSpec((1, GROUP), row)],
        out_specs=pl.BlockSpec((1, tm, GROUP), lambda bi, i: (bi, i, 0)),
        scratch_shapes=[pltpu.VMEM((CONV_HALO + tm, GROUP), F32),
                        pltpu.VMEM((SUBLANES - 1, CONV_HALO + tm - SUBLANES, GROUP), F32)],
        compiler_params=_cparams(("parallel", "parallel")),
        name="conv_mixer",
    )(u32, u32, u32, u32, w, bias, ln_g, ln_b)


def _pool_conv_kernel(pu_ref, ph_ref, pw_ref, ps_ref, a_ref, g_ref, ha_ref, hg_ref, w_ref, b_ref, lng_ref,
                      lnb_ref, op_ref, oc_ref, h_ref, hs_ref, *, tm, rows):
    _pool_kernel(pu_ref, ph_ref, pw_ref, ps_ref, op_ref, tm=tm)
    _conv_kernel(a_ref, g_ref, ha_ref, hg_ref, w_ref, b_ref, lng_ref, lnb_ref, oc_ref, h_ref, hs_ref,
                 tm=tm, rows=rows)


def _pool_conv_mixer(u32, pool_bd, pool_scale, w, bias, ln_g, ln_b, tm):
    b, s, _ = u32.shape
    rows = min(tm, 64)
    row = lambda bi, i: (0, 0)
    halo = lambda col, h: pl.BlockSpec((1, h, GROUP), lambda bi, i: (bi, jnp.maximum(i * (tm // h) - 1, 0), col))
    tile = lambda col: pl.BlockSpec((1, tm, GROUP), lambda bi, i: (bi, i, col))
    out = pl.BlockSpec((1, tm, GROUP), lambda bi, i: (bi, i, 0))
    vec = pl.BlockSpec((1, GROUP), row)
    return pl.pallas_call(
        functools.partial(_pool_conv_kernel, tm=tm, rows=rows),
        out_shape=(jax.ShapeDtypeStruct((b, s, GROUP), BF16),) * 2,
        grid=(b, s // tm),
        in_specs=[tile(COL_POOL), halo(COL_POOL, POOL_HALO), pl.BlockSpec((GROUP, GROUP), row), vec,
                  tile(COL_CA), tile(COL_CG), halo(COL_CA, CONV_HALO), halo(COL_CG, CONV_HALO),
                  pl.BlockSpec((CONV_HALO, GROUP), row), vec, vec, vec],
        out_specs=(out, out),
        scratch_shapes=[pltpu.VMEM((CONV_HALO + tm, GROUP), F32),
                        pltpu.VMEM((SUBLANES - 1, CONV_HALO + tm - SUBLANES, GROUP), F32)],
        compiler_params=_cparams(("parallel", "parallel")),
        name="pool_conv_mixer",
    )(u32, u32, pool_bd, pool_scale, u32, u32, u32, u32, w, bias, ln_g, ln_b)


def _ret_kernel(q_ref, k_ref, v_ref, g_ref, cos_ref, sin_ref, gn_ref, o_ref, state_ref, raw_ref, *, tr):
    C = RET_CHUNK

    @pl.when(pl.program_id(1) == 0)
    def _():
        state_ref[...] = jnp.zeros(state_ref.shape, F32)

    lane = lax.broadcasted_iota(jnp.int32, (1, LANES), 1)
    lo = lane < HEAD_DIM
    first_half = (lane % HEAD_DIM) < (HEAD_DIM // 2)
    ri = lax.broadcasted_iota(jnp.int32, (C, C), 0)
    ci = lax.broadcasted_iota(jnp.int32, (C, C), 1)
    diff = (ri - ci).astype(F32)
    tok = lax.broadcasted_iota(jnp.int32, (C, 1), 0).astype(F32)
    same_head = (ri < HEAD_DIM) == (ci < HEAD_DIM)
    gi = lax.broadcasted_iota(jnp.int32, (GROUP, GROUP), 0) // HEAD_DIM
    gj = lax.broadcasted_iota(jnp.int32, (GROUP, GROUP), 1) // HEAD_DIM
    seg_mean = jnp.where(gi == gj, 1.0 / HEAD_DIM, 0.0).astype(BF16)

    def seg_mean_dot(x):
        hi = x.astype(BF16)
        lo_part = (x - hi.astype(F32)).astype(BF16)
        return _dot(hi, seg_mean) + _dot(lo_part, seg_mean)

    cos = cos_ref[0]
    sin = sin_ref[0]

    def rope(x):
        swapped = jnp.where(first_half, pltpu.roll(x, LANES - HEAD_DIM // 2, 1), pltpu.roll(x, HEAD_DIM // 2, 1))
        return x * cos + swapped * sin

    for pr in range(RET_HEADS // 2):
        lg = [math.log(1.0 - 2.0 ** (-5.0 - (2 * pr + hh))) for hh in range(2)]
        dmask = [jnp.where(diff >= 0, jnp.exp(lg[hh] * jnp.maximum(diff, 0.0)), 0.0) for hh in range(2)]
        zeta = jnp.where(lo, jnp.exp(lg[0] * (C - 1 - tok)), jnp.exp(lg[1] * (C - 1 - tok)))
        xi = jnp.where(lo, jnp.exp(lg[0] * (tok + 1)), jnp.exp(lg[1] * (tok + 1)))
        row_is_lo = lax.broadcasted_iota(jnp.int32, (LANES, 1), 0) < HEAD_DIM
        decay = jnp.where(row_is_lo, math.exp(lg[0] * C), math.exp(lg[1] * C))
        cols = slice(pr * LANES, (pr + 1) * LANES)
        qr = rope(q_ref[0, :, cols]).astype(BF16)
        kf = rope(k_ref[0, :, cols]) * (HEAD_DIM ** -0.5)
        chunks = [slice(n * C, (n + 1) * C) for n in range(tr // C)]
        kb = kf.astype(BF16)
        vb = v_ref[0, :, cols].astype(BF16)
        zero = jnp.zeros_like(qr)
        q_heads = (jnp.where(lo, qr, zero), jnp.where(lo, zero, qr))
        scores = [[(_dot_nt(q_heads[hh][rows], kb[rows]) * dmask[hh]).astype(BF16) for hh in range(2)]
                  for rows in chunks]
        kvs = [_dot_tn((kf[rows] * zeta).astype(BF16), vb[rows]) for rows in chunks]
        intra = [[_dot(sc[hh], vb[rows]) for hh in range(2)] for sc, rows in zip(scores, chunks)]
        state = state_ref[pr]
        for n, rows in enumerate(chunks):
            cross = _dot(qr[rows], state.astype(BF16)) * xi
            raw_ref[rows, cols] = jnp.where(lo, intra[n][0], intra[n][1]) + cross
            state = state * decay + jnp.where(same_head, kvs[n], 0.0)
        state_ref[pr] = state

    o = raw_ref[...]
    d = o - seg_mean_dot(o)
    var = seg_mean_dot(d * d)
    y = d * lax.rsqrt(var + EPS) * gn_ref[...]
    o_ref[0] = (_silu(g_ref[0]) * y).astype(BF16)


def _retention(u32, cos_t, sin_t, gn_g, tr):
    b, s, _ = u32.shape
    blk = lambda col: pl.BlockSpec((1, tr, GROUP), lambda bi, i: (bi, i, col))
    tab = pl.BlockSpec((1, tr, LANES), lambda bi, i: (bi, i, 0))
    return pl.pallas_call(
        functools.partial(_ret_kernel, tr=tr),
        out_shape=jax.ShapeDtypeStruct((b, s, GROUP), BF16),
        grid=(b, s // tr),
        in_specs=[blk(COL_RQ), blk(COL_RK), blk(COL_RV), blk(COL_RG), tab, tab,
                  pl.BlockSpec((1, GROUP), lambda bi, i: (0, 0))],
        out_specs=pl.BlockSpec((1, tr, GROUP), lambda bi, i: (bi, i, 0)),
        scratch_shapes=[pltpu.VMEM((RET_HEADS // 2, LANES, LANES), F32), pltpu.VMEM((tr, GROUP), F32)],
        compiler_params=_cparams(("parallel", "arbitrary")),
        name="retention",
    )(u32, u32, u32, u32, cos_t, sin_t, gn_g)


def _dense_kernel(x_ref, yf_ref, yp_ref, yr_ref, yc_ref, mod_ref, g_ref, fg_ref,
                  wo_ref, w1_ref, w3_ref, w2_ref, o_ref, *, ff_chunk, final):
    mix = _dot(yf_ref[0], wo_ref[0:GROUP, :])
    for n, y_ref in enumerate((yp_ref, yr_ref, yc_ref), start=1):
        mix = mix + _dot(y_ref[0], wo_ref[n * GROUP:(n + 1) * GROUP, :])
    x = x_ref[0] + mod_ref[0, 2:3, :] * mix
    h = _rms_mod(x, g_ref[...], mod_ref[0, 4:5, :], mod_ref[0, 3:4, :]).astype(BF16)
    d_ff = w1_ref.shape[1]
    f = None
    for c in range(0, d_ff, ff_chunk):
        a = _dot(h, w1_ref[:, c:c + ff_chunk])
        gated = (_silu(a) * _dot(h, w3_ref[:, c:c + ff_chunk])).astype(BF16)
        part = _dot(gated, w2_ref[c:c + ff_chunk, :])
        f = part if f is None else f + part
    x = x + mod_ref[0, 5:6, :] * f
    if final:
        ms = jnp.mean(x * x, axis=-1, keepdims=True)
        x = x * lax.rsqrt(ms + EPS) * fg_ref[...]
    o_ref[0] = x


def _dense(x, ys, mod, g, final_g, wo, w1, w3, w2, l, tm, final):
    b, s, d = x.shape
    d_ff = w1.shape[-1]
    ff_chunk = 256
    assert d_ff % ff_chunk == 0
    resident = lambda arr: _layer_spec(arr, l, pipeline_mode=pl.Buffered(1))
    tile = lambda w: pl.BlockSpec((1, tm, w), lambda bi, i: (bi, i, 0))
    return pl.pallas_call(
        functools.partial(_dense_kernel, ff_chunk=ff_chunk, final=final),
        out_shape=jax.ShapeDtypeStruct((b, s, d), F32),
        grid=(b, s // tm),
        in_specs=[tile(d), tile(GROUP), tile(GROUP), tile(GROUP), tile(GROUP),
                  pl.BlockSpec((None, 1, 6, d), lambda bi, i: (l, bi, 0, 0)),
                  _layer_spec(g, l), pl.BlockSpec((1, d), lambda bi, i: (0, 0)),
                  resident(wo), resident(w1), resident(w3), resident(w2)],
        out_specs=tile(d),
        compiler_params=_cparams(("parallel", "parallel")),
        name="out_proj_swiglu",
    )(x, *ys, mod, g, final_g, wo, w1, w3, w2)


def kernel(x, c, positions, ada_w, ada_b, norm_mix_g, norm_ffn_g, w_in, fox_fb, pool_w, pool_scale,
           ret_gn_g, conv_w, conv_b, conv_ln_g, conv_ln_b, w_out, ffn_w1, ffn_w3, ffn_w2, final_g):
    b, s, d = x.shape
    depth = ada_w.shape[0]
    tm = min(s, 512)
    t_mix = min(s, 1024)
    n_fox = 3 * GROUP
    n_ff = fox_fb.shape[1]

    mod = _ada_mod(c, ada_w, ada_b).reshape(depth, b, 6, d)
    cos_t, sin_t = _rope_tables(positions)

    w16 = w_in[:, :, :2 * GROUP].astype(BF16)
    wvt = jnp.swapaxes(w_in[:, :, 2 * GROUP:n_fox], 1, 2).astype(BF16)
    w32 = jnp.concatenate(
        [w_in[:, :, n_fox + n_ff:], w_in[:, :, n_fox:n_fox + n_ff],
         jnp.zeros((depth, d, LANES - n_ff), F32)], axis=2).astype(BF16)
    assert w32.shape[-1] == U32_WIDTH
    wo, w1, w3, w2 = (w.astype(BF16) for w in (w_out, ffn_w1, ffn_w3, ffn_w2))
    g_mix = norm_mix_g.reshape(depth, 1, d)
    g_ffn = norm_ffn_g.reshape(depth, 1, d)

    for l in range(depth):
        u16, vt, u32 = _inproj(x, mod, g_mix, w16, wvt, w32, l, tm)

        fb_row = jnp.pad(fox_fb[l], (0, LANES - n_ff)).reshape(1, LANES)
        kx = _forget_keys(u32, u16, fb_row)
        y_fox = _fox_attention(u16, kx, vt, tm)

        pool_bd = jax.scipy.linalg.block_diag(*[pool_w[l][gi] for gi in range(len(POOL_WINDOWS))]).astype(BF16)
        y_ret = _retention(u32, cos_t, sin_t, ret_gn_g[l].reshape(1, GROUP), t_mix)

        conv_w_pad = jnp.pad(conv_w[l], ((0, CONV_HALO - CONV_WIDTH), (0, 0)))
        y_pool, y_conv = _pool_conv_mixer(
            u32, pool_bd, pool_scale[l].reshape(1, GROUP), conv_w_pad, conv_b[l].reshape(1, GROUP),
            conv_ln_g[l].reshape(1, GROUP), conv_ln_b[l].reshape(1, GROUP), t_mix)

        x = _dense(x, (y_fox, y_pool, y_ret, y_conv), mod, g_ffn, final_g.reshape(1, d),
                   wo, w1, w3, w2, l, tm, final=(l == depth - 1))
    return x
```
